```python
import math
import jax, jax.numpy as jnp
from jax import lax
import numpy as np

D_MODEL = 4096
BATCH = 4
SEQ = 2048
DEPTH = 2
DEC_BATCH = 128
DEC_SEQ = 8
PAST_LEN = 16384
PAGE_SIZE = 128

N_META = 16
CHUNK = 64
CONV_W = 4
NORM_EPS = 1e-6
M_HEADS = 4
M_DV = D_MODEL // 16
M_DK = M_DV // 2
M_WIDTH = M_HEADS * M_DV
R_WIDTH = D_MODEL // 4
R_BLOCKS = 8
R_BDIM = R_WIDTH // R_BLOCKS
R_C = 8.0
G_DK = 128
G_DV = 128
G_WIDTH = D_MODEL // 2
G_HEADS = G_WIDTH // G_DV
G_QKV = 2 * G_HEADS * G_DK + G_WIDTH
D_MIX = M_WIDTH + R_WIDTH + G_WIDTH
D_FF = -(-8 * D_MODEL // (3 * 256)) * 256
SIZES = (M_HEADS * M_DK, M_HEADS * M_DK, M_WIDTH, M_WIDTH, M_HEADS, M_HEADS,
         R_WIDTH, R_WIDTH,
         G_HEADS * G_DK, G_HEADS * G_DK, G_WIDTH, G_WIDTH, G_HEADS, G_HEADS)
D_IN = sum(SIZES)
SPLITS = tuple(sum(SIZES[:i + 1]) for i in range(len(SIZES) - 1))

kernel_name = 'hymba_mlstm_rglru_gdn_step'


def f32(a):
    return a.astype(jnp.float32)


def rmsnorm(x, w):
    xf = f32(x)
    y = xf * lax.rsqrt(jnp.mean(xf * xf, axis=-1, keepdims=True) + NORM_EPS)
    return (y * f32(w)).astype(x.dtype)


def l2norm(x):
    return x * lax.rsqrt(jnp.sum(x * x, axis=-1, keepdims=True) + NORM_EPS)


def causal_conv(x, buf, w):
    xp = jnp.concatenate([buf.astype(x.dtype), x], axis=1)
    y = lax.conv_general_dilated(xp, w.astype(x.dtype)[:, None, :], (1,), 'VALID',
                                 dimension_numbers=('NWC', 'WIO', 'NWC'),
                                 feature_group_count=x.shape[-1])
    return y, xp[:, -(CONV_W - 1):]


def to_heads(a, h, d):
    b, t, _ = a.shape
    return a.reshape(b, t, h, d).transpose(0, 2, 1, 3)


def chunked_scan(step, carry, seqs, chunk):
    t = seqs[0].shape[2]
    n = t // chunk
    xs = tuple(jnp.moveaxis(a.reshape(a.shape[:2] + (n, chunk) + a.shape[3:]), 2, 0) for a in seqs)
    carry, ys = lax.scan(step, carry, xs)
    ys = jnp.moveaxis(ys, 0, 2)
    return carry, ys.reshape(ys.shape[:2] + (t,) + ys.shape[4:])


def run_chunked(step, carry, seqs, n_lead):
    if n_lead:
        carry, y0 = chunked_scan(step, carry, tuple(a[:, :, :n_lead] for a in seqs), n_lead)
        rest = tuple(a[:, :, n_lead:] for a in seqs)
        carry, y1 = chunked_scan(step, carry, rest, math.gcd(rest[0].shape[2], CHUNK))
        return carry, jnp.concatenate([y0, y1], axis=2)
    return chunked_scan(step, carry, seqs, math.gcd(seqs[0].shape[2], CHUNK))


def mlstm_chunk(carry, inp):
    c_st, n_st, m_st = carry
    q, k, v, li, lf = inp
    L = q.shape[2]
    tri = jnp.tril(jnp.ones((L, L), dtype=bool))
    F = jnp.cumsum(lf, axis=-1)
    D = jnp.where(tri, F[..., :, None] - F[..., None, :] + li[..., None, :], -jnp.inf)
    inter = m_st[..., None] + F
    m_t = jnp.maximum(jnp.max(D, axis=-1), inter)
    w_inter = jnp.exp(inter - m_t)
    s = jnp.einsum('bhtd,bhsd->bhts', q, k) * jnp.exp(D - m_t[..., None])
    num = w_inter[..., None] * jnp.einsum('bhtd,bhde->bhte', q, c_st) + jnp.einsum('bhts,bhse->bhte', s, v)
    den = w_inter * jnp.einsum('bhtd,bhd->bht', q, n_st) + jnp.sum(s, axis=-1)
    h = num / jnp.maximum(jnp.abs(den), jnp.exp(-m_t))[..., None]
    m_new = m_t[..., -1]
    w_end = jnp.exp(F[..., -1:] - F + li - m_new[..., None])
    scale_prev = jnp.exp(m_st + F[..., -1] - m_new)
    c_new = scale_prev[..., None, None] * c_st + jnp.einsum('bhs,bhsd,bhse->bhde', w_end, k, v)
    n_new = scale_prev[..., None] * n_st + jnp.einsum('bhs,bhsd->bhd', w_end, k)
    return (c_new, n_new, m_new), h


def gdn_chunk(S, inp):
    q, k, v, beta, g = inp
    L = q.shape[2]
    tri = jnp.tril(jnp.ones((L, L), dtype=bool))
    strict = jnp.tril(jnp.ones((L, L), dtype=bool), -1)
    G = jnp.cumsum(g, axis=-1)
    diff = G[..., :, None] - G[..., None, :]
    decay = jnp.where(tri, jnp.exp(jnp.where(tri, diff, 0.0)), 0.0)
    kb = k * beta[..., None]
    A = jnp.where(strict, jnp.einsum('bhtd,bhsd->bhts', kb, k) * decay, 0.0)
    eye = jnp.eye(L, dtype=A.dtype)
    rhs = jnp.concatenate([v * beta[..., None], kb * jnp.exp(G)[..., None]], axis=-1)
    sol = lax.linalg.triangular_solve(A + eye, rhs, left_side=True, lower=True, unit_diagonal=True)
    u, w = sol[..., :G_DV], sol[..., G_DV:]
    v_new = u - jnp.einsum('bhtd,bhde->bhte', w, S)
    attn = jnp.einsum('bhtd,bhsd->bhts', q, k) * decay
    o = jnp.einsum('bhtd,bhde->bhte', q * jnp.exp(G)[..., None], S) + jnp.einsum('bhts,bhse->bhte', attn, v_new)
    g_end = G[..., -1]
    S_new = S * jnp.exp(g_end)[..., None, None] + jnp.einsum(
        'bhsd,bhse->bhde', k * jnp.exp(g_end[..., None] - G)[..., None], v_new)
    return S_new, o


def mlstm_mixer(mq, mk, mv, mo, mi, mf, b_i, b_f, norm_w, c0, n0, m0, n_lead):
    b, t, _ = mq.shape
    q = to_heads(f32(mq), M_HEADS, M_DK) * (M_DK ** -0.5)
    k = to_heads(f32(mk), M_HEADS, M_DK)
    v = to_heads(f32(mv), M_HEADS, M_DV)
    li = jnp.swapaxes(f32(mi) + f32(b_i), 1, 2)
    lf = jnp.swapaxes(jax.nn.log_sigmoid(f32(mf) + f32(b_f)), 1, 2)
    (c, n, m), h = run_chunked(mlstm_chunk, (f32(c0), f32(n0), f32(m0)), (q, k, v, li, lf), n_lead)
    h = rmsnorm(h.transpose(0, 2, 1, 3), norm_w.reshape(M_HEADS, M_DV))
    h = h * jax.nn.sigmoid(f32(mo)).reshape(b, t, M_HEADS, M_DV)
    return h.reshape(b, t, M_WIDTH), c, n, m


def rglru_mixer(rx, rg, conv_w, conv_b, wa, ba, wx, bx, lam, h0, buf):
    b, t, _ = rx.shape
    xc, buf_new = causal_conv(rx, buf, conv_w)
    xc = f32(xc) + f32(conv_b)
    xb = xc.reshape(b, t, R_BLOCKS, R_BDIM)
    r = jax.nn.sigmoid(jnp.einsum('btnd,nde->btne', xb, f32(wa)).reshape(b, t, R_WIDTH) + f32(ba))
    i = jax.nn.sigmoid(jnp.einsum('btnd,nde->btne', xb, f32(wx)).reshape(b, t, R_WIDTH) + f32(bx))
    log_a = -R_C * r * jax.nn.softplus(-f32(lam))
    a = jnp.exp(log_a)
    u = jnp.sqrt(-jnp.expm1(2.0 * log_a)) * (i * xc)

    def step(h, au):
        h = au[0] * h + au[1]
        return h, h

    h_last, hs = lax.scan(step, f32(h0), (jnp.swapaxes(a, 0, 1), jnp.swapaxes(u, 0, 1)))
    y = jnp.swapaxes(hs, 0, 1) * jax.nn.gelu(f32(rg))
    return y, h_last, buf_new


def gdn_mixer(gq, gk, gv, gz, gb, ga, conv_w, a_log, dt_bias, norm_w, s0, buf, n_lead):
    b, t, _ = gq.shape
    qkv, buf_new = causal_conv(jnp.concatenate([gq, gk, gv], axis=-1), buf, conv_w)
    qkv = jax.nn.silu(f32(qkv))
    q, k, v = jnp.split(qkv, (G_HEADS * G_DK, 2 * G_HEADS * G_DK), axis=-1)
    q = l2norm(to_heads(q, G_HEADS, G_DK)) * (G_DK ** -0.5)
    k = l2norm(to_heads(k, G_HEADS, G_DK))
    v = to_heads(v, G_HEADS, G_DV)
    beta = jnp.swapaxes(jax.nn.sigmoid(f32(gb)), 1, 2)
    g = jnp.swapaxes(-jnp.exp(f32(a_log)) * jax.nn.softplus(f32(ga) + f32(dt_bias)), 1, 2)
    S, o = run_chunked(gdn_chunk, f32(s0), (q, k, v, beta, g), n_lead)
    o = rmsnorm(o.transpose(0, 2, 1, 3), norm_w) * jax.nn.silu(f32(gz)).reshape(b, t, G_HEADS, G_DV)
    return o.reshape(b, t, G_WIDTH), S, buf_new


def trunk(x, st_c, st_n, st_m, st_h, st_rconv, st_s, st_gconv, n_lead, p):
    (norm_mix, w_in, m_bias_i, m_bias_f, m_norm, r_conv_w, r_conv_b, r_gate_a_w, r_gate_a_b,
     r_gate_x_w, r_gate_x_b, r_lambda, g_conv_w, g_A_log, g_dt_bias, g_norm, w_out,
     norm_ffn, w_gate, w_up, w_down, norm_final) = p
    new = [[] for _ in range(7)]
    for l in range(DEPTH):
        hn = rmsnorm(x, norm_mix[l])
        proj = hn @ w_in[l]
        (mq, mk, mv, mo, mi, mf, rx, rg, gq, gk, gv, gz, gb, ga) = jnp.split(proj, SPLITS, axis=-1)
        ym, c, n, m = mlstm_mixer(mq, mk, mv, mo, mi, mf, m_bias_i[l], m_bias_f[l], m_norm[l],
                                  st_c[l], st_n[l], st_m[l], n_lead)
        yr, h, rbuf = rglru_mixer(rx, rg, r_conv_w[l], r_conv_b[l], r_gate_a_w[l], r_gate_a_b[l],
                                  r_gate_x_w[l], r_gate_x_b[l], r_lambda[l], st_h[l], st_rconv[l])
        yg, s, gbuf = gdn_mixer(gq, gk, gv, gz, gb, ga, g_conv_w[l], g_A_log[l], g_dt_bias[l],
                                g_norm[l], st_s[l], st_gconv[l], n_lead)
        mix = jnp.concatenate([ym, yr, yg], axis=-1).astype(x.dtype)
        x = x + mix @ w_out[l]
        hf = rmsnorm(x, norm_ffn[l])
        x = x + (jax.nn.silu(hf @ w_gate[l]) * (hf @ w_up[l])) @ w_down[l]
        for lst, val in zip(new, (c, n, m, h, rbuf, s, gbuf)):
            lst.append(val)
    y = rmsnorm(x, norm_final)
    return y, [jnp.stack(v) for v in new]


def setup_inputs(seed: int = 0) -> dict:
    key = jax.random.key(seed)
    ks = iter(jax.random.split(key, 48))

    def nrm(shape, s):
        return jax.random.normal(next(ks), shape, jnp.float32) * s

    def unif(shape, lo, hi):
        return jax.random.uniform(next(ks), shape, jnp.float32, lo, hi)

    a_rg = unif((DEPTH, R_WIDTH), 0.9, 0.999) ** (1.0 / R_C)
    dt = jnp.exp(unif((DEPTH, G_HEADS), math.log(1e-3), math.log(0.1)))
    return {
        'x_prompt': nrm((BATCH, SEQ, D_MODEL), 1.0),
        'x_sample': nrm((DEC_BATCH, DEC_SEQ, D_MODEL), 1.0),
        'state_mlstm_C': nrm((DEPTH, DEC_BATCH, M_HEADS, M_DK, M_DV), 0.1),
        'state_mlstm_n': nrm((DEPTH, DEC_BATCH, M_HEADS, M_DK), 0.1),
        'state_mlstm_m': nrm((DEPTH, DEC_BATCH, M_HEADS), 1.0),
        'state_rglru_h': nrm((DEPTH, DEC_BATCH, R_WIDTH), 0.5),
        'state_rglru_conv': nrm((DEPTH, DEC_BATCH, CONV_W - 1, R_WIDTH), 1.0),
        'state_gdn_S': nrm((DEPTH, DEC_BATCH, G_HEADS, G_DK, G_DV), 0.1),
        'state_gdn_conv': nrm((DEPTH, DEC_BATCH, CONV_W - 1, G_QKV), 1.0),
        'meta_tokens': nrm((N_META, D_MODEL), 1.0),
        'norm_mix': 1.0 + nrm((DEPTH, D_MODEL), 0.02),
        'w_in': nrm((DEPTH, D_MODEL, D_IN), D_MODEL ** -0.5),
        'm_bias_i': nrm((DEPTH, M_HEADS), 0.1),
        'm_bias_f': unif((DEPTH, M_HEADS), 3.0, 6.0),
        'm_norm': 1.0 + nrm((DEPTH, M_WIDTH), 0.02),
        'r_conv_w': nrm((DEPTH, CONV_W, R_WIDTH), CONV_W ** -0.5),
        'r_conv_b': nrm((DEPTH, R_WIDTH), 0.02),
        'r_gate_a_w': nrm((DEPTH, R_BLOCKS, R_BDIM, R_BDIM), R_BDIM ** -0.5),
        'r_gate_a_b': nrm((DEPTH, R_WIDTH), 0.02),
        'r_gate_x_w': nrm((DEPTH, R_BLOCKS, R_BDIM, R_BDIM), R_BDIM ** -0.5),
        'r_gate_x_b': nrm((DEPTH, R_WIDTH), 0.02),
        'r_lambda': jnp.log(a_rg) - jnp.log1p(-a_rg),
        'g_conv_w': nrm((DEPTH, CONV_W, G_QKV), CONV_W ** -0.5),
        'g_A_log': jnp.log(unif((DEPTH, G_HEADS), 1.0, 16.0)),
        'g_dt_bias': dt + jnp.log(-jnp.expm1(-dt)),
        'g_norm': 1.0 + nrm((DEPTH, G_DV), 0.02),
        'w_out': nrm((DEPTH, D_MIX, D_MODEL), D_MIX ** -0.5),
        'norm_ffn': 1.0 + nrm((DEPTH, D_MODEL), 0.02),
        'w_gate': nrm((DEPTH, D_MODEL, D_FF), D_MODEL ** -0.5),
        'w_up': nrm((DEPTH, D_MODEL, D_FF), D_MODEL ** -0.5),
        'w_down': nrm((DEPTH, D_FF, D_MODEL), D_FF ** -0.5),
        'norm_final': 1.0 + nrm((D_MODEL,), 0.02),
    }


def reference(x_prompt, x_sample, state_mlstm_C, state_mlstm_n, state_mlstm_m, state_rglru_h,
              state_rglru_conv, state_gdn_S, state_gdn_conv, meta_tokens, norm_mix, w_in,
              m_bias_i, m_bias_f, m_norm, r_conv_w, r_conv_b, r_gate_a_w, r_gate_a_b,
              r_gate_x_w, r_gate_x_b, r_lambda, g_conv_w, g_A_log, g_dt_bias, g_norm, w_out,
              norm_ffn, w_gate, w_up, w_down, norm_final):
    p = (norm_mix, w_in, m_bias_i, m_bias_f, m_norm, r_conv_w, r_conv_b, r_gate_a_w, r_gate_a_b,
         r_gate_x_w, r_gate_x_b, r_lambda, g_conv_w, g_A_log, g_dt_bias, g_norm, w_out,
         norm_ffn, w_gate, w_up, w_down, norm_final)
    b = x_prompt.shape[0]
    meta = jnp.broadcast_to(meta_tokens.astype(x_prompt.dtype)[None], (b, N_META, D_MODEL))
    xp = jnp.concatenate([meta, x_prompt], axis=1)
    f = jnp.float32
    yp, sp = trunk(xp,
                   jnp.zeros((DEPTH, b, M_HEADS, M_DK, M_DV), f),
                   jnp.zeros((DEPTH, b, M_HEADS, M_DK), f),
                   jnp.zeros((DEPTH, b, M_HEADS), f),
                   jnp.zeros((DEPTH, b, R_WIDTH), f),
                   jnp.zeros((DEPTH, b, CONV_W - 1, R_WIDTH), x_prompt.dtype),
                   jnp.zeros((DEPTH, b, G_HEADS, G_DK, G_DV), f),
                   jnp.zeros((DEPTH, b, CONV_W - 1, G_QKV), x_prompt.dtype),
                   N_META, p)
    y_prompt = yp[:, N_META:]
    y_sample, ss = trunk(x_sample, state_mlstm_C, state_mlstm_n, state_mlstm_m, state_rglru_h,
                         state_rglru_conv, state_gdn_S, state_gdn_conv, 0, p)
    return (y_prompt, y_sample, sp[0], sp[1], sp[2], sp[3], sp[4], sp[5], sp[6],
            ss[0], ss[1], ss[2], ss[3], ss[4], ss[5], ss[6])
```

```python
import functools
import math

import jax
import jax.numpy as jnp
from jax import lax
from jax.experimental import pallas as pl
from jax.experimental.pallas import tpu as pltpu

F32 = jnp.float32
BF16 = jnp.bfloat16
HIGHEST = lax.Precision.HIGHEST

D_MODEL = 4096
N_META = 16
CONV_W = 4
NORM_EPS = 1e-6
M_HEADS = 4
M_DV = 256
M_DK = 128
M_WIDTH = M_HEADS * M_DV
R_WIDTH = 1024
R_BLOCKS = 8
R_BDIM = R_WIDTH // R_BLOCKS
R_C = 8.0
G_DK = 128
G_DV = 128
G_HEADS = 16
G_WIDTH = G_HEADS * G_DV
G_QKV = 3 * G_WIDTH
D_FF = 11008

_SRC = dict(mq=0, mk=512, mv=1024, mo=2048, mi=3072, mf=3076, rx=3080, rg=4104,
            gq=5128, gk=7176, gv=9224, gz=11272, gb=13320, ga=13336, end=13352)
COL_GQ, COL_GK, COL_GV, COL_GZ = 0, 2048, 4096, 6144
COL_MV, COL_MO, COL_RX, COL_RG, COL_MQ, COL_MK = 8192, 9216, 10240, 11264, 12288, 12800
N_MAIN = 13312
LANES = 128
GATE_MI, GATE_MF, GATE_GB, GATE_GA = 0, 4, 8, 24

SUBLANES = 8
VMEM_LIMIT = 56 * 2 ** 20


def _params(semantics):
    return pltpu.CompilerParams(dimension_semantics=semantics, vmem_limit_bytes=VMEM_LIMIT)


def _dot(a, b):
    return jnp.dot(a.astype(BF16), b.astype(BF16), preferred_element_type=F32)


def _dot_nt(a, b):
    return lax.dot_general(a.astype(BF16), b.astype(BF16), (((1,), (1,)), ((), ())),
                           preferred_element_type=F32)


def _dot_tn(a, b):
    return lax.dot_general(a.astype(BF16), b.astype(BF16), (((0,), (0,)), ((), ())),
                           preferred_element_type=F32)


def _dot_f32(a, b):
    return jnp.dot(a, b, precision=HIGHEST, preferred_element_type=F32)


def _softplus(x):
    return jnp.maximum(x, 0.0) + jnp.log1p(jnp.exp(-jnp.abs(x)))


def _silu(x):
    return x * jax.nn.sigmoid(x)


def _seq_masks(nb, L):
    R = nb * L
    row = lax.broadcasted_iota(jnp.int32, (R, R), 0)
    col = lax.broadcasted_iota(jnp.int32, (R, R), 1)
    causal = row >= col
    strict = row > col
    if nb > 1:
        same = (row // L) == (col // L)
        causal = jnp.logical_and(causal, same)
        strict = jnp.logical_and(strict, same)
    return causal, strict


def _conv4(x, hist, w, nb, L):
    R = nb * L
    acc = x * w[CONV_W - 1:CONV_W, :]
    if L == SUBLANES:
        tpos = lax.broadcasted_iota(jnp.int32, (R, 1), 0) % L
    else:
        tpos = lax.broadcasted_iota(jnp.int32, (SUBLANES, 1), 0)
    for j in range(1, CONV_W):
        xs = pltpu.roll(x, j, 0)
        shift = (j - (CONV_W - 1)) % (nb * SUBLANES)
        hr = hist if shift == 0 else pltpu.roll(hist, shift, 0)
        if L == SUBLANES:
            sh = jnp.where(tpos < j, hr, xs)
        else:
            parts = []
            for s in range(nb):
                head = jnp.where(tpos < j, hr[s * SUBLANES:(s + 1) * SUBLANES],
                                 xs[s * L:s * L + SUBLANES])
                parts += [head, xs[s * L + SUBLANES:(s + 1) * L]]
            sh = jnp.concatenate(parts, axis=0)
        acc = acc + sh * w[CONV_W - 1 - j:CONV_W - j, :]
    return acc


def _next_hist(x, nb, L):
    if L == SUBLANES:
        return pltpu.roll(x, nb * L - (SUBLANES - (CONV_W - 1)), 0)
    parts = [pltpu.roll(x[(s + 1) * L - SUBLANES:(s + 1) * L], CONV_W - 1, 0) for s in range(nb)]
    return parts[0] if nb == 1 else jnp.concatenate(parts, axis=0)


def _for_each_seq(nb, fn):
    if nb <= 4:
        for s in range(nb):
            fn(s)
    else:
        def body(s, carry):
            fn(s)
            return carry
        lax.fori_loop(0, nb, body, 0)


def _rows(s, L):
    start = s * L
    if not isinstance(start, int):
        start = pl.multiple_of(start, SUBLANES)
    return pl.ds(start, L)


def _rmsnorm_kernel(x_ref, w_ref, o_ref):
    x = x_ref[...]
    y = x * lax.rsqrt(jnp.mean(x * x, axis=-1, keepdims=True) + NORM_EPS)
    o_ref[...] = (y * w_ref[...]).astype(o_ref.dtype)


def _rmsnorm(x, w, out_dtype, tm=464):
    n, d = x.shape
    return pl.pallas_call(
        _rmsnorm_kernel,
        grid=(n // tm,),
        in_specs=[pl.BlockSpec((tm, d), lambda i: (i, 0)), pl.BlockSpec((1, d), lambda i: (0, 0))],
        out_specs=pl.BlockSpec((tm, d), lambda i: (i, 0)),
        out_shape=jax.ShapeDtypeStruct((n, d), out_dtype),
        compiler_params=_params(("parallel",)),
        name="rmsnorm",
    )(x, w.reshape(1, d))


def _mm_kernel(a_ref, w_ref, o_ref):
    o_ref[...] = jnp.dot(a_ref[...], w_ref[...], preferred_element_type=F32).astype(o_ref.dtype)


def _matmul(a, w, tm, tn, name):
    m, k = a.shape
    n = w.shape[1]
    return pl.pallas_call(
        _mm_kernel,
        grid=(m // tm, n // tn),
        in_specs=[pl.BlockSpec((tm, k), lambda i, j: (i, 0)), pl.BlockSpec((k, tn), lambda i, j: (0, j))],
        out_specs=pl.BlockSpec((tm, tn), lambda i, j: (i, j)),
        out_shape=jax.ShapeDtypeStruct((m, n), F32),
        compiler_params=_params(("parallel", "arbitrary")),
        name=name,
    )(a, w)


def _mm_swiglu_kernel(a_ref, wg_ref, wu_ref, o_ref):
    a = a_ref[...]
    g = jnp.dot(a, wg_ref[...], preferred_element_type=F32)
    u = jnp.dot(a, wu_ref[...], preferred_element_type=F32)
    o_ref[...] = (_silu(g) * u).astype(o_ref.dtype)


def _matmul_swiglu(a, wg, wu, tm, tn):
    m, k = a.shape
    n = wg.shape[1]
    return pl.pallas_call(
        _mm_swiglu_kernel,
        grid=(m // tm, n // tn),
        in_specs=[pl.BlockSpec((tm, k), lambda i, j: (i, 0)),
                  pl.BlockSpec((k, tn), lambda i, j: (0, j)),
                  pl.BlockSpec((k, tn), lambda i, j: (0, j))],
        out_specs=pl.BlockSpec((tm, tn), lambda i, j: (i, j)),
        out_shape=jax.ShapeDtypeStruct((m, n), BF16),
        compiler_params=_params(("parallel", "arbitrary")),
        name="ffn_gate_up",
    )(a, wg, wu)


def _mm_resid_kernel(widths, *refs):
    a_refs = refs[:len(widths)]
    w_ref, x_ref, o_ref = refs[len(widths):]
    acc = x_ref[...]
    k0 = 0
    for a_ref, kw in zip(a_refs, widths):
        acc = acc + jnp.dot(a_ref[...], w_ref[k0:k0 + kw, :], preferred_element_type=F32)
        k0 += kw
    o_ref[...] = acc


def _matmul_resid(a_list, w, x, tm, tn, name):
    m = x.shape[0]
    k, n = w.shape
    widths = tuple(a.shape[1] for a in a_list)
    return pl.pallas_call(
        functools.partial(_mm_resid_kernel, widths),
        grid=(m // tm, n // tn),
        in_specs=[pl.BlockSpec((tm, kw), lambda i, j: (i, 0)) for kw in widths]
        + [pl.BlockSpec((k, tn), lambda i, j: (0, j)), pl.BlockSpec((tm, tn), lambda i, j: (i, j))],
        out_specs=pl.BlockSpec((tm, tn), lambda i, j: (i, j)),
        out_shape=jax.ShapeDtypeStruct((m, n), F32),
        compiler_params=_params(("parallel", "arbitrary")),
        name=name,
    )(*a_list, w, x)


def _rglru_kernel(nb, L, rx_ref, rg_ref, hist0_ref, h0_ref, cw_ref, cb_ref, wa_ref, ba_ref, wx_ref,
                  bx_ref, lam_ref, y_ref, h_ref, hist_ref, a_s, u_s):
    @pl.when(pl.program_id(1) == 0)
    def _():
        h_ref[...] = h0_ref[...]
        hist_ref[...] = hist0_ref[...]

    x = rx_ref[...]
    xc = _conv4(x, hist_ref[...], cw_ref[...], nb, L) + cb_ref[...]
    hist_ref[...] = _next_hist(x, nb, L)

    xcb = xc.astype(BF16)
    ga = jnp.concatenate([jnp.dot(xcb[:, n * R_BDIM:(n + 1) * R_BDIM], wa_ref[n], preferred_element_type=F32)
                          for n in range(R_BLOCKS)], axis=1)
    gx = jnp.concatenate([jnp.dot(xcb[:, n * R_BDIM:(n + 1) * R_BDIM], wx_ref[n], preferred_element_type=F32)
                          for n in range(R_BLOCKS)], axis=1)
    r = jax.nn.sigmoid(ga + ba_ref[...])
    i = jax.nn.sigmoid(gx + bx_ref[...])
    log_a = -R_C * r * _softplus(-lam_ref[...])
    a_s[...] = jnp.exp(log_a)
    th = jnp.tanh(log_a)
    u_s[...] = jnp.sqrt(-2.0 * th / (1.0 - th)) * (i * xc)

    if nb == 1:
        def step(t, h):
            h = a_s[pl.ds(t, 1), :] * h + u_s[pl.ds(t, 1), :]
            u_s[pl.ds(t, 1), :] = h
            return h

        h_last = lax.fori_loop(0, L, step, h_ref[0:1, :], unroll=SUBLANES)
        h_ref[...] = jnp.broadcast_to(h_last, (SUBLANES, R_WIDTH))
        hs = u_s[...]
    else:
        a, u = a_s[...], u_s[...]
        tpos = lax.broadcasted_iota(jnp.int32, (nb * L, 1), 0) % L
        step = 1
        while step < L:
            valid = tpos >= step
            u = jnp.where(valid, a * pltpu.roll(u, step, 0) + u, u)
            a = jnp.where(valid, a * pltpu.roll(a, step, 0), a)
            step *= 2
        if L == SUBLANES:
            h0 = h_ref[...]
        else:
            h0 = jnp.concatenate([jnp.broadcast_to(h_ref[s * SUBLANES:s * SUBLANES + 1, :], (L, R_WIDTH))
                                  for s in range(nb)], axis=0)
        hs = a * h0 + u
        if L == SUBLANES:
            h_ref[...] = pltpu.roll(hs, nb * L - (L - 1), 0)
        else:
            h_ref[...] = jnp.concatenate([jnp.broadcast_to(hs[(s + 1) * L - 1:(s + 1) * L, :], (SUBLANES, R_WIDTH))
                                          for s in range(nb)], axis=0)
    y_ref[...] = (hs * jax.nn.gelu(rg_ref[...])).astype(y_ref.dtype)


def _rglru_group(pm, row0, nseq, nb, L, nch, hist0, h0, prm):
    R = nb * L
    base = row0 // R
    nsb = nseq // nb
    rb = lambda sb, c: base + sb * nch + c
    full = lambda shape: pl.BlockSpec(shape, lambda sb, c: (0,) * len(shape))
    st = pl.BlockSpec((nb * SUBLANES, R_WIDTH), lambda sb, c: (sb, 0))
    return pl.pallas_call(
        functools.partial(_rglru_kernel, nb, L),
        grid=(nsb, nch),
        in_specs=[pl.BlockSpec((R, R_WIDTH), lambda sb, c: (rb(sb, c), COL_RX // R_WIDTH)),
                  pl.BlockSpec((R, R_WIDTH), lambda sb, c: (rb(sb, c), COL_RG // R_WIDTH)),
                  st, st,
                  full((CONV_W, R_WIDTH)), full((1, R_WIDTH)),
                  full((R_BLOCKS, R_BDIM, R_BDIM)), full((1, R_WIDTH)),
                  full((R_BLOCKS, R_BDIM, R_BDIM)), full((1, R_WIDTH)), full((1, R_WIDTH))],
        out_specs=[pl.BlockSpec((R, R_WIDTH), lambda sb, c: (sb * nch + c, 0)), st, st],
        out_shape=[jax.ShapeDtypeStruct((nseq * nch * L, R_WIDTH), BF16),
                   jax.ShapeDtypeStruct((nseq * SUBLANES, R_WIDTH), F32),
                   jax.ShapeDtypeStruct((nseq * SUBLANES, R_WIDTH), F32)],
        scratch_shapes=[pltpu.VMEM((R, R_WIDTH), F32), pltpu.VMEM((R, R_WIDTH), F32)],
        compiler_params=_params(("parallel", "arbitrary")),
        name="rglru",
    )(pm, pm, hist0, h0, *prm)


def _mlstm_kernel(nb, L, q_ref, k_ref, v_ref, o_ref, g_ref, bias_ref, nw_ref, c0_ref, n0_ref, m0_ref,
                  y_ref, c_ref, n_ref, m_ref, num_s, col_s):
    R = nb * L

    @pl.when(pl.program_id(1) == 0)
    def _():
        c_ref[...] = c0_ref[...]
        n_ref[...] = n0_ref[...]
        m_ref[...] = m0_ref[...]

    z = g_ref[...] + bias_ref[...]
    lf = pltpu.roll(-_softplus(-z), LANES - GATE_MF, 1)
    causal, _ = _seq_masks(nb, L)
    fcum = _dot_f32(causal.astype(F32), lf)
    bcol = z - fcum
    brow = bcol.T
    if L == SUBLANES:
        m_rows = m_ref[...]
    else:
        m_rows = jnp.concatenate(
            [jnp.broadcast_to(m_ref[s * SUBLANES:s * SUBLANES + 1, :], (L, LANES)) for s in range(nb)], axis=0)

    for h in range(M_HEADS):
        fc = fcum[:, h:h + 1]
        d = fc + brow[h:h + 1, :]
        rowmax = jnp.max(jnp.where(causal, d, -jnp.inf), axis=-1, keepdims=True)
        inter = m_rows[:, h:h + 1] + fc
        mt = jnp.maximum(rowmax, inter)
        p = jnp.where(causal, jnp.exp(jnp.where(causal, d - mt, 0.0)), 0.0)
        qh = q_ref[:, h * M_DK:(h + 1) * M_DK] * (M_DK ** -0.5)
        s = _dot_nt(qh, k_ref[:, h * M_DK:(h + 1) * M_DK]) * p
        num_s[:, h * M_DV:(h + 1) * M_DV] = _dot(s, v_ref[:, h * M_DV:(h + 1) * M_DV])
        for idx, val in enumerate((mt, jnp.exp(inter - mt), jnp.sum(s, axis=-1, keepdims=True),
                                   bcol[:, h:h + 1], fc)):
            col_s[idx, h] = jnp.broadcast_to(val, (R, LANES))

    def per_seq(s):
        rows = _rows(s, L)
        mrow = pl.ds(s * SUBLANES, 1)
        for h in range(M_HEADS):
            dk = slice(h * M_DK, (h + 1) * M_DK)
            dv = slice(h * M_DV, (h + 1) * M_DV)
            mt = col_s[0, h, rows, 0:1]
            w_inter = col_s[1, h, rows, 0:1]
            den_in = col_s[2, h, rows, 0:1]
            bc = col_s[3, h, rows, 0:1]
            f_last = col_s[4, h, rows, 0:1][L - 1:L]
            qh = q_ref[rows, dk] * (M_DK ** -0.5)
            kh = k_ref[rows, dk]
            vh = v_ref[rows, dv]
            c_old = c_ref[s, h]
            n_old = n_ref[s, h:h + 1, :]
            m_old = m_ref[mrow, h:h + 1]
            num = w_inter * _dot(qh, c_old) + num_s[rows, dv]
            den = w_inter * jnp.sum(qh * n_old, axis=-1, keepdims=True) + den_in
            hh = num / jnp.maximum(jnp.abs(den), jnp.exp(-mt))
            hn = hh * lax.rsqrt(jnp.mean(hh * hh, axis=-1, keepdims=True) + NORM_EPS) * nw_ref[:, dv]
            y_ref[rows, dv] = (hn * jax.nn.sigmoid(o_ref[rows, dv])).astype(y_ref.dtype)
            m_new = mt[L - 1:L]
            kw = kh * jnp.exp(bc + (f_last - m_new))
            scale = jnp.exp(m_old + f_last - m_new)
            c_ref[s, h] = scale * c_old + _dot_tn(kw, vh)
            n_ref[s, h:h + 1, :] = scale * n_old + jnp.sum(kw, axis=0, keepdims=True)
            m_ref[pl.ds(s * SUBLANES, SUBLANES), h:h + 1] = jnp.broadcast_to(m_new, (SUBLANES, 1))

    _for_each_seq(nb, per_seq)


def _mlstm_group(pm, pg, row0, nseq, nb, L, nch, c0, n0, m0, bias_row, norm_row):
    R = nb * L
    base = row0 // R
    nsb = nseq // nb
    rb = lambda sb, c: base + sb * nch + c
    full = lambda shape: pl.BlockSpec(shape, lambda sb, c: (0,) * len(shape))
    cst = pl.BlockSpec((nb, M_HEADS, M_DK, M_DV), lambda sb, c: (sb, 0, 0, 0))
    nst = pl.BlockSpec((nb, M_HEADS, M_DK), lambda sb, c: (sb, 0, 0))
    mst = pl.BlockSpec((nb * SUBLANES, LANES), lambda sb, c: (sb, 0))
    return pl.pallas_call(
        functools.partial(_mlstm_kernel, nb, L),
        grid=(nsb, nch),
        in_specs=[pl.BlockSpec((R, M_HEADS * M_DK), lambda sb, c: (rb(sb, c), COL_MQ // (M_HEADS * M_DK))),
                  pl.BlockSpec((R, M_HEADS * M_DK), lambda sb, c: (rb(sb, c), COL_MK // (M_HEADS * M_DK))),
                  pl.BlockSpec((R, M_WIDTH), lambda sb, c: (rb(sb, c), COL_MV // M_WIDTH)),
                  pl.BlockSpec((R, M_WIDTH), lambda sb, c: (rb(sb, c), COL_MO // M_WIDTH)),
                  pl.BlockSpec((R, LANES), lambda sb, c: (rb(sb, c), 0)),
                  full((1, LANES)), full((1, M_WIDTH)), cst, nst, mst],
        out_specs=[pl.BlockSpec((R, M_WIDTH), lambda sb, c: (sb * nch + c, 0)), cst, nst, mst],
        out_shape=[jax.ShapeDtypeStruct((nseq * nch * L, M_WIDTH), BF16),
                   jax.ShapeDtypeStruct((nseq, M_HEADS, M_DK, M_DV), F32),
                   jax.ShapeDtypeStruct((nseq, M_HEADS, M_DK), F32),
                   jax.ShapeDtypeStruct((nseq * SUBLANES, LANES), F32)],
        scratch_shapes=[pltpu.VMEM((R, M_WIDTH), F32), pltpu.VMEM((5, M_HEADS, R, LANES), F32)],
        compiler_params=_params(("parallel", "arbitrary")),
        name="mlstm",
    )(pm, pm, pm, pm, pg, bias_row, norm_row, c0, n0, m0)


def _unit_lower_inverse(a, L):
    R = a.shape[0]
    eye = (lax.broadcasted_iota(jnp.int32, (R, R), 0) == lax.broadcasted_iota(jnp.int32, (R, R), 1)).astype(F32)
    p = eye - a
    b = _dot_f32(a, a)
    levels = int(math.log2(L)) - 1
    for lvl in range(levels):
        if lvl + 1 < levels:
            x = _dot_f32(b, jnp.concatenate([b, p], axis=1))
            b = x[:, :R]
            p = p + x[:, R:]
        else:
            p = p + _dot_f32(b, p)
    return p


def _gdn_kernel(nb, L, q_ref, k_ref, v_ref, z_ref, g_ref, cw_ref, hist0_ref, alog_ref, dtb_ref, nw_ref,
                s0_ref, y_ref, s_ref, hist_ref, k_s, kq_s, ws_s, vn_s, g_s):
    R = nb * L

    @pl.when(pl.program_id(1) == 0)
    def _():
        s_ref[...] = s0_ref[...]
        hist_ref[...] = hist0_ref[...]

    zt = g_ref[...]
    beta = pltpu.roll(jax.nn.sigmoid(zt), LANES - GATE_GB, 1)
    gdec = pltpu.roll(-jnp.exp(alog_ref[...]) * _softplus(zt + dtb_ref[...]), LANES - GATE_GA, 1)
    causal, strict = _seq_masks(nb, L)
    gcum = _dot_f32(causal.astype(F32), gdec)
    g_s[...] = gcum
    grow = gcum.T

    for h in range(G_HEADS):
        sl = slice(h * G_DK, (h + 1) * G_DK)
        parts = []
        for idx, ref in enumerate((q_ref, k_ref, v_ref)):
            cs = slice(idx * G_WIDTH + h * G_DK, idx * G_WIDTH + (h + 1) * G_DK)
            parts.append(_silu(_conv4(ref[:, sl], hist_ref[:, cs], cw_ref[:, cs], nb, L)))
        qh, kh, vh = parts
        q = qh * lax.rsqrt(jnp.sum(qh * qh, axis=-1, keepdims=True) + NORM_EPS) * (G_DK ** -0.5)
        k = kh * lax.rsqrt(jnp.sum(kh * kh, axis=-1, keepdims=True) + NORM_EPS)
        bc = beta[:, h:h + 1]
        gc = gcum[:, h:h + 1]
        decay = jnp.where(causal, jnp.exp(jnp.where(causal, gc - grow[h:h + 1, :], 0.0)), 0.0)
        kb = k * bc
        aa = _dot_nt(jnp.concatenate([kb, q], axis=0), k)
        a = jnp.where(strict, aa[:R] * decay, 0.0)
        attn = aa[R:] * decay
        tinv = _unit_lower_inverse(a, L)
        eg = jnp.exp(gc)
        k_s[...] = k
        kq_s[0] = kb * eg
        kq_s[1] = q * eg

        def read_state(s):
            rows = _rows(s, L)
            x = _dot(jnp.concatenate([kq_s[0, rows, :], kq_s[1, rows, :]], axis=0), s_ref[s, h])
            ws_s[0, rows, :] = x[:L]
            ws_s[1, rows, :] = x[L:]

        _for_each_seq(nb, read_state)
        v_new = _dot_f32(tinv, vh * bc - ws_s[0])
        vn_s[...] = v_new
        o = ws_s[1] + _dot(attn, v_new)

        def write_state(s):
            rows = _rows(s, L)
            gseq = g_s[rows, h:h + 1]
            g_end = gseq[L - 1:L]
            kdec = k_s[rows, :] * jnp.exp(g_end - gseq)
            s_ref[s, h] = s_ref[s, h] * jnp.exp(g_end) + _dot_tn(kdec, vn_s[rows, :])

        _for_each_seq(nb, write_state)
        on = o * lax.rsqrt(jnp.mean(o * o, axis=-1, keepdims=True) + NORM_EPS) * nw_ref[...]
        y_ref[:, sl] = (on * _silu(z_ref[:, sl])).astype(y_ref.dtype)

    for idx, ref in enumerate((q_ref, k_ref, v_ref)):
        hist_ref[:, idx * G_WIDTH:(idx + 1) * G_WIDTH] = _next_hist(ref[...], nb, L)


def _gdn_group(pm, pg, row0, nseq, nb, L, nch, s0, hist0, prm):
    R = nb * L
    base = row0 // R
    nsb = nseq // nb
    rb = lambda sb, c: base + sb * nch + c
    full = lambda shape: pl.BlockSpec(shape, lambda sb, c: (0,) * len(shape))
    sst = pl.BlockSpec((nb, G_HEADS, G_DK, G_DV), lambda sb, c: (sb, 0, 0, 0))
    hst = pl.BlockSpec((nb * SUBLANES, G_QKV), lambda sb, c: (sb, 0))
    col = lambda off: pl.BlockSpec((R, G_WIDTH), lambda sb, c: (rb(sb, c), off // G_WIDTH))
    return pl.pallas_call(
        functools.partial(_gdn_kernel, nb, L),
        grid=(nsb, nch),
        in_specs=[col(COL_GQ), col(COL_GK), col(COL_GV), col(COL_GZ),
                  pl.BlockSpec((R, LANES), lambda sb, c: (rb(sb, c), 0)),
                  full((CONV_W, G_QKV)), hst, full((1, LANES)), full((1, LANES)), full((1, G_DV)), sst],
        out_specs=[pl.BlockSpec((R, G_WIDTH), lambda sb, c: (sb * nch + c, 0)), sst, hst],
        out_shape=[jax.ShapeDtypeStruct((nseq * nch * L, G_WIDTH), BF16),
                   jax.ShapeDtypeStruct((nseq, G_HEADS, G_DK, G_DV), F32),
                   jax.ShapeDtypeStruct((nseq * SUBLANES, G_QKV), F32)],
        scratch_shapes=[pltpu.VMEM((R, G_DK), F32), pltpu.VMEM((2, R, G_DK), F32),
                        pltpu.VMEM((2, R, G_DV), F32), pltpu.VMEM((R, G_DV), F32),
                        pltpu.VMEM((R, LANES), F32)],
        compiler_params=_params(("parallel", "arbitrary")),
        name="gdn",
    )(pm, pm, pm, pm, pg, prm[0], hist0, prm[1], prm[2], prm[3], s0)


def _lane_row(vec, offset):
    return jnp.zeros((1, LANES), F32).at[0, offset:offset + vec.shape[0]].set(vec.astype(F32))


def _pad_hist(buf):
    n, w, c = buf.shape
    return jnp.pad(buf.astype(F32), ((0, 0), (0, SUBLANES - w), (0, 0))).reshape(n * SUBLANES, c)


def _unpad_hist(hist, n):
    return hist.reshape(n, SUBLANES, hist.shape[-1])[:, :CONV_W - 1]


def _mixers(pm, pg, group, states, lw):
    row0, nseq, nb_m, nb_r, nb_g, L_m, L_r, L_g, T = group
    c0, n0, m0, h0, rhist, s0, ghist = states
    ym, c, n, m = _mlstm_group(pm, pg, row0, nseq, nb_m, L_m, T // L_m, c0, n0, m0, lw["m_bias"], lw["m_norm"])
    yr, h, rh = _rglru_group(pm, row0, nseq, nb_r, L_r, T // L_r, rhist, h0, lw["r_prm"])
    yg, s, gh = _gdn_group(pm, pg, row0, nseq, nb_g, L_g, T // L_g, s0, ghist, lw["g_prm"])
    return (ym, yr, yg), (c, n, m, h, rh, s, gh)


def kernel(x_prompt, x_sample, state_mlstm_C, state_mlstm_n, state_mlstm_m, state_rglru_h, state_rglru_conv, state_gdn_S, state_gdn_conv, meta_tokens, norm_mix, w_in, m_bias_i, m_bias_f, m_norm, r_conv_w, r_conv_b, r_gate_a_w, r_gate_a_b, r_gate_x_w, r_gate_x_b, r_lambda, g_conv_w, g_A_log, g_dt_bias, g_norm, w_out, norm_ffn, w_gate, w_up, w_down, norm_final):
    batch, seq, d = x_prompt.shape
    dec_batch, dec_seq, _ = x_sample.shape
    depth = w_in.shape[0]
    n_prompt = batch * seq
    n_sample = dec_batch * dec_seq
    n_rows = n_prompt + n_sample + batch * N_META

    meta = jnp.broadcast_to(meta_tokens.astype(F32)[None], (batch, N_META, d))
    x = jnp.concatenate([x_prompt.reshape(n_prompt, d), x_sample.reshape(n_sample, d),
                         meta.reshape(batch * N_META, d)], axis=0)

    g_meta = (n_prompt + n_sample, batch, batch, batch, batch, N_META, N_META, N_META, N_META)
    g_prompt = (0, batch, 1, 1, 1, 256, 256, 64, seq)
    g_sample = (n_prompt, dec_batch, 16, 32, 8, dec_seq, dec_seq, dec_seq, dec_seq)

    zeros = lambda *shape: jnp.zeros(shape, F32)
    meta_states = (zeros(batch, M_HEADS, M_DK, M_DV), zeros(batch, M_HEADS, M_DK), zeros(batch * SUBLANES, LANES),
                   zeros(batch * SUBLANES, R_WIDTH), zeros(batch * SUBLANES, R_WIDTH),
                   zeros(batch, G_HEADS, G_DK, G_DV), zeros(batch * SUBLANES, G_QKV))

    prompt_new, sample_new = [], []
    for l in range(depth):
        w = w_in[l]
        w_main = jnp.concatenate([w[:, _SRC["gq"]:_SRC["gb"]], w[:, _SRC["mv"]:_SRC["mi"]],
                                  w[:, _SRC["rx"]:_SRC["gq"]], w[:, _SRC["mq"]:_SRC["mv"]]], axis=1).astype(BF16)
        w_gates = jnp.concatenate([w[:, _SRC["mi"]:_SRC["rx"]], w[:, _SRC["gb"]:_SRC["end"]],
                                   jnp.zeros((d, LANES - 40), w.dtype)], axis=1).astype(BF16)
        lw = dict(
            m_bias=_lane_row(m_bias_i[l], GATE_MI) + _lane_row(m_bias_f[l], GATE_MF),
            m_norm=m_norm[l].reshape(1, M_WIDTH),
            r_prm=(r_conv_w[l], r_conv_b[l].reshape(1, R_WIDTH), r_gate_a_w[l].astype(BF16),
                   r_gate_a_b[l].reshape(1, R_WIDTH), r_gate_x_w[l].astype(BF16),
                   r_gate_x_b[l].reshape(1, R_WIDTH), r_lambda[l].reshape(1, R_WIDTH)),
            g_prm=(g_conv_w[l], _lane_row(g_A_log[l], GATE_GA), _lane_row(g_dt_bias[l], GATE_GA),
                   g_norm[l].reshape(1, G_DV)),
        )
        sample_states = (state_mlstm_C[l], state_mlstm_n[l],
                         jnp.pad(jnp.repeat(state_mlstm_m[l], SUBLANES, axis=0), ((0, 0), (0, LANES - M_HEADS))),
                         jnp.repeat(state_rglru_h[l], SUBLANES, axis=0), _pad_hist(state_rglru_conv[l]),
                         state_gdn_S[l], _pad_hist(state_gdn_conv[l]))

        hn = _rmsnorm(x, norm_mix[l], BF16)
        pm = _matmul(hn, w_main, 1856, 512, "in_proj")
        pg = _matmul(hn, w_gates, 1856, LANES, "in_proj_gates")

        y_meta, st_meta = _mixers(pm, pg, g_meta, meta_states, lw)
        y_prompt, st_prompt = _mixers(pm, pg, g_prompt, st_meta, lw)
        y_sample, st_sample = _mixers(pm, pg, g_sample, sample_states, lw)
        mix = [jnp.concatenate([y_prompt[i], y_sample[i], y_meta[i]], axis=0) for i in range(3)]
        prompt_new.append(st_prompt)
        sample_new.append(st_sample)

        x = _matmul_resid(mix, w_out[l].astype(BF16), x, 1856, 256, "out_proj")
        hf = _rmsnorm(x, norm_ffn[l], BF16)
        hmid = _matmul_swiglu(hf, w_gate[l].astype(BF16), w_up[l].astype(BF16), 1856, 256)
        x = _matmul_resid([hmid], w_down[l].astype(BF16), x, 464, 512, "ffn_down")

    y = _rmsnorm(x, norm_final, F32)
    y_prompt_out = y[:n_prompt].reshape(batch, seq, d)
    y_sample_out = y[n_prompt:n_prompt + n_sample].reshape(dec_batch, dec_seq, d)

    def unpack(per_layer, n):
        c, nn, m, h, rh, s, gh = (jnp.stack([st[i] for st in per_layer]) for i in range(7))
        return (c, nn, m[:, ::SUBLANES, :M_HEADS], h[:, ::SUBLANES],
                jax.vmap(lambda a: _unpad_hist(a, n))(rh), s, jax.vmap(lambda a: _unpad_hist(a, n))(gh))

    return (y_prompt_out, y_sample_out) + unpack(prompt_new, batch) + unpack(sample_new, dec_batch)
```

```python
import functools
import math

import jax
import jax.numpy as jnp
from jax import lax
from jax.experimental import pallas as pl
from jax.experimental.pallas import tpu as pltpu

F32 = jnp.float32
BF16 = jnp.bfloat16
HIGHEST = lax.Precision.HIGHEST

D_MODEL = 4096
N_META = 16
CONV_W = 4
NORM_EPS = 1e-6
M_HEADS = 4
M_DV = 256
M_DK = 128
M_WIDTH = M_HEADS * M_DV
R_WIDTH = 1024
R_BLOCKS = 8
R_BDIM = R_WIDTH // R_BLOCKS
R_C = 8.0
G_DK = 128
G_DV = 128
G_HEADS = 16
G_WIDTH = G_HEADS * G_DV
G_QKV = 3 * G_WIDTH
D_FF = 11008

_SRC = dict(mq=0, mk=512, mv=1024, mo=2048, mi=3072, mf=3076, rx=3080, rg=4104,
            gq=5128, gk=7176, gv=9224, gz=11272, gb=13320, ga=13336, end=13352)
COL_GQ, COL_GK, COL_GV, COL_GZ = 0, 2048, 4096, 6144
COL_MV, COL_MO, COL_RX, COL_RG, COL_MQ, COL_MK = 8192, 9216, 10240, 11264, 12288, 12800
N_MAIN = 13312
LANES = 128
GATE_MI, GATE_MF, GATE_GB, GATE_GA = 0, 4, 8, 24

SUBLANES = 8
VMEM_LIMIT = 56 * 2 ** 20


def _params(semantics):
    return pltpu.CompilerParams(dimension_semantics=semantics, vmem_limit_bytes=VMEM_LIMIT)


def _dot(a, b):
    return jnp.dot(a.astype(BF16), b.astype(BF16), preferred_element_type=F32)


def _dot_nt(a, b):
    return lax.dot_general(a.astype(BF16), b.astype(BF16), (((1,), (1,)), ((), ())),
                           preferred_element_type=F32)


def _dot_tn(a, b):
    return lax.dot_general(a.astype(BF16), b.astype(BF16), (((0,), (0,)), ((), ())),
                           preferred_element_type=F32)


def _dot_f32(a, b):
    return jnp.dot(a, b, precision=HIGHEST, preferred_element_type=F32)


def _softplus(x):
    return jnp.maximum(x, 0.0) + jnp.log1p(jnp.exp(-jnp.abs(x)))


def _silu(x):
    return x * jax.nn.sigmoid(x)


def _seq_masks(nb, L):
    R = nb * L
    row = lax.broadcasted_iota(jnp.int32, (R, R), 0)
    col = lax.broadcasted_iota(jnp.int32, (R, R), 1)
    causal = row >= col
    strict = row > col
    if nb > 1:
        same = (row // L) == (col // L)
        causal = jnp.logical_and(causal, same)
        strict = jnp.logical_and(strict, same)
    return causal, strict


def _cumsum_rows(x, nb, L):
    tpos = lax.broadcasted_iota(jnp.int32, (nb * L, 1), 0) % L
    step = 1
    while step < L:
        x = x + jnp.where(tpos >= step, pltpu.roll(x, step, 0), 0.0)
        step *= 2
    return x


def _conv4(x, hist, w, nb, L):
    R = nb * L
    acc = x * w[CONV_W - 1:CONV_W, :]
    if L == SUBLANES:
        tpos = lax.broadcasted_iota(jnp.int32, (R, 1), 0) % L
    else:
        tpos = lax.broadcasted_iota(jnp.int32, (SUBLANES, 1), 0)
    for j in range(1, CONV_W):
        xs = pltpu.roll(x, j, 0)
        shift = (j - (CONV_W - 1)) % (nb * SUBLANES)
        hr = hist if shift == 0 else pltpu.roll(hist, shift, 0)
        if L == SUBLANES:
            sh = jnp.where(tpos < j, hr, xs)
        else:
            parts = []
            for s in range(nb):
                head = jnp.where(tpos < j, hr[s * SUBLANES:(s + 1) * SUBLANES],
                                 xs[s * L:s * L + SUBLANES])
                parts += [head, xs[s * L + SUBLANES:(s + 1) * L]]
            sh = jnp.concatenate(parts, axis=0)
        acc = acc + sh * w[CONV_W - 1 - j:CONV_W - j, :]
    return acc


def _next_hist(x, nb, L):
    if L == SUBLANES:
        return pltpu.roll(x, nb * L - (SUBLANES - (CONV_W - 1)), 0)
    parts = [pltpu.roll(x[(s + 1) * L - SUBLANES:(s + 1) * L], CONV_W - 1, 0) for s in range(nb)]
    return parts[0] if nb == 1 else jnp.concatenate(parts, axis=0)


def _for_each_seq(nb, fn):
    if nb <= 4:
        for s in range(nb):
            fn(s)
    else:
        def body(s, carry):
            fn(s)
            return carry
        lax.fori_loop(0, nb, body, 0)


def _rows(s, L):
    start = s * L
    if not isinstance(start, int):
        start = pl.multiple_of(start, SUBLANES)
    return pl.ds(start, L)


def _rmsnorm_kernel(x_ref, w_ref, o_ref):
    x = x_ref[...]
    y = x * lax.rsqrt(jnp.mean(x * x, axis=-1, keepdims=True) + NORM_EPS)
    o_ref[...] = (y * w_ref[...]).astype(o_ref.dtype)


def _rmsnorm(x, w, out_dtype, tm=464):
    n, d = x.shape
    return pl.pallas_call(
        _rmsnorm_kernel,
        grid=(n // tm,),
        in_specs=[pl.BlockSpec((tm, d), lambda i: (i, 0)), pl.BlockSpec((1, d), lambda i: (0, 0))],
        out_specs=pl.BlockSpec((tm, d), lambda i: (i, 0)),
        out_shape=jax.ShapeDtypeStruct((n, d), out_dtype),
        compiler_params=_params(("parallel",)),
        name="rmsnorm",
    )(x, w.reshape(1, d))


def _mm_kernel(a_ref, w_ref, o_ref):
    o_ref[...] = jnp.dot(a_ref[...], w_ref[...], preferred_element_type=F32).astype(o_ref.dtype)


def _matmul(a, w, tm, tn, name):
    m, k = a.shape
    n = w.shape[1]
    return pl.pallas_call(
        _mm_kernel,
        grid=(m // tm, n // tn),
        in_specs=[pl.BlockSpec((tm, k), lambda i, j: (i, 0)), pl.BlockSpec((k, tn), lambda i, j: (0, j))],
        out_specs=pl.BlockSpec((tm, tn), lambda i, j: (i, j)),
        out_shape=jax.ShapeDtypeStruct((m, n), F32),
        compiler_params=_params(("parallel", "arbitrary")),
        name=name,
    )(a, w)


def _mm_swiglu_kernel(a_ref, wg_ref, wu_ref, o_ref):
    a = a_ref[...]
    g = jnp.dot(a, wg_ref[...], preferred_element_type=F32)
    u = jnp.dot(a, wu_ref[...], preferred_element_type=F32)
    o_ref[...] = (_silu(g) * u).astype(o_ref.dtype)


def _matmul_swiglu(a, wg, wu, tm, tn):
    m, k = a.shape
    n = wg.shape[1]
    return pl.pallas_call(
        _mm_swiglu_kernel,
        grid=(m // tm, n // tn),
        in_specs=[pl.BlockSpec((tm, k), lambda i, j: (i, 0)),
                  pl.BlockSpec((k, tn), lambda i, j: (0, j)),
                  pl.BlockSpec((k, tn), lambda i, j: (0, j))],
        out_specs=pl.BlockSpec((tm, tn), lambda i, j: (i, j)),
        out_shape=jax.ShapeDtypeStruct((m, n), BF16),
        compiler_params=_params(("parallel", "arbitrary")),
        name="ffn_gate_up",
    )(a, wg, wu)


def _mm_resid_kernel(widths, *refs):
    a_refs = refs[:len(widths)]
    w_ref, x_ref, o_ref = refs[len(widths):]
    acc = x_ref[...]
    k0 = 0
    for a_ref, kw in zip(a_refs, widths):
        acc = acc + jnp.dot(a_ref[...], w_ref[k0:k0 + kw, :], preferred_element_type=F32)
        k0 += kw
    o_ref[...] = acc


def _matmul_resid(a_list, w, x, tm, tn, name):
    m = x.shape[0]
    k, n = w.shape
    widths = tuple(a.shape[1] for a in a_list)
    return pl.pallas_call(
        functools.partial(_mm_resid_kernel, widths),
        grid=(m // tm, n // tn),
        in_specs=[pl.BlockSpec((tm, kw), lambda i, j: (i, 0)) for kw in widths]
        + [pl.BlockSpec((k, tn), lambda i, j: (0, j)), pl.BlockSpec((tm, tn), lambda i, j: (i, j))],
        out_specs=pl.BlockSpec((tm, tn), lambda i, j: (i, j)),
        out_shape=jax.ShapeDtypeStruct((m, n), F32),
        compiler_params=_params(("parallel", "arbitrary")),
        name=name,
    )(*a_list, w, x)


def _rglru_kernel(nb, L, rx_ref, rg_ref, hist0_ref, h0_ref, cw_ref, cb_ref, wa_ref, ba_ref, wx_ref,
                  bx_ref, lam_ref, y_ref, h_ref, hist_ref, a_s, u_s):
    @pl.when(pl.program_id(1) == 0)
    def _():
        h_ref[...] = h0_ref[...]
        hist_ref[...] = hist0_ref[...]

    x = rx_ref[...]
    xc = _conv4(x, hist_ref[...], cw_ref[...], nb, L) + cb_ref[...]
    hist_ref[...] = _next_hist(x, nb, L)

    xcb = xc.astype(BF16)
    ga = jnp.concatenate([jnp.dot(xcb[:, n * R_BDIM:(n + 1) * R_BDIM], wa_ref[n], preferred_element_type=F32)
                          for n in range(R_BLOCKS)], axis=1)
    gx = jnp.concatenate([jnp.dot(xcb[:, n * R_BDIM:(n + 1) * R_BDIM], wx_ref[n], preferred_element_type=F32)
                          for n in range(R_BLOCKS)], axis=1)
    r = jax.nn.sigmoid(ga + ba_ref[...])
    i = jax.nn.sigmoid(gx + bx_ref[...])
    log_a = -R_C * r * _softplus(-lam_ref[...])
    a_s[...] = jnp.exp(log_a)
    th = jnp.tanh(log_a)
    u_s[...] = jnp.sqrt(-2.0 * th / (1.0 - th)) * (i * xc)

    if nb == 1:
        def step(t, h):
            h = a_s[pl.ds(t, 1), :] * h + u_s[pl.ds(t, 1), :]
            u_s[pl.ds(t, 1), :] = h
            return h

        h_last = lax.fori_loop(0, L, step, h_ref[0:1, :], unroll=SUBLANES)
        h_ref[...] = jnp.broadcast_to(h_last, (SUBLANES, R_WIDTH))
        hs = u_s[...]
    else:
        a, u = a_s[...], u_s[...]
        tpos = lax.broadcasted_iota(jnp.int32, (nb * L, 1), 0) % L
        step = 1
        while step < L:
            valid = tpos >= step
            u = jnp.where(valid, a * pltpu.roll(u, step, 0) + u, u)
            a = jnp.where(valid, a * pltpu.roll(a, step, 0), a)
            step *= 2
        if L == SUBLANES:
            h0 = h_ref[...]
        else:
            h0 = jnp.concatenate([jnp.broadcast_to(h_ref[s * SUBLANES:s * SUBLANES + 1, :], (L, R_WIDTH))
                                  for s in range(nb)], axis=0)
        hs = a * h0 + u
        if L == SUBLANES:
            h_ref[...] = pltpu.roll(hs, nb * L - (L - 1), 0)
        else:
            h_ref[...] = jnp.concatenate([jnp.broadcast_to(hs[(s + 1) * L - 1:(s + 1) * L, :], (SUBLANES, R_WIDTH))
                                          for s in range(nb)], axis=0)
    y_ref[...] = (hs * jax.nn.gelu(rg_ref[...])).astype(y_ref.dtype)


def _rglru_group(pm, row0, nseq, nb, L, nch, hist0, h0, prm):
    R = nb * L
    base = row0 // R
    nsb = nseq // nb
    rb = lambda sb, c: base + sb * nch + c
    full = lambda shape: pl.BlockSpec(shape, lambda sb, c: (0,) * len(shape))
    st = pl.BlockSpec((nb * SUBLANES, R_WIDTH), lambda sb, c: (sb, 0))
    return pl.pallas_call(
        functools.partial(_rglru_kernel, nb, L),
        grid=(nsb, nch),
        in_specs=[pl.BlockSpec((R, R_WIDTH), lambda sb, c: (rb(sb, c), COL_RX // R_WIDTH)),
                  pl.BlockSpec((R, R_WIDTH), lambda sb, c: (rb(sb, c), COL_RG // R_WIDTH)),
                  st, st,
                  full((CONV_W, R_WIDTH)), full((1, R_WIDTH)),
                  full((R_BLOCKS, R_BDIM, R_BDIM)), full((1, R_WIDTH)),
                  full((R_BLOCKS, R_BDIM, R_BDIM)), full((1, R_WIDTH)), full((1, R_WIDTH))],
        out_specs=[pl.BlockSpec((R, R_WIDTH), lambda sb, c: (sb * nch + c, 0)), st, st],
        out_shape=[jax.ShapeDtypeStruct((nseq * nch * L, R_WIDTH), BF16),
                   jax.ShapeDtypeStruct((nseq * SUBLANES, R_WIDTH), F32),
                   jax.ShapeDtypeStruct((nseq * SUBLANES, R_WIDTH), F32)],
        scratch_shapes=[pltpu.VMEM((R, R_WIDTH), F32), pltpu.VMEM((R, R_WIDTH), F32)],
        compiler_params=_params(("parallel", "arbitrary")),
        name="rglru",
    )(pm, pm, hist0, h0, *prm)


def _mlstm_kernel(nb, L, q_ref, k_ref, v_ref, o_ref, g_ref, bias_ref, nw_ref, c0_ref, n0_ref, m0_ref,
                  y_ref, c_ref, n_ref, m_ref, num_s, col_s):
    R = nb * L

    @pl.when(pl.program_id(1) == 0)
    def _():
        c_ref[...] = c0_ref[...]
        n_ref[...] = n0_ref[...]
        m_ref[...] = m0_ref[...]

    z = g_ref[...] + bias_ref[...]
    lf = pltpu.roll(-_softplus(-z), LANES - GATE_MF, 1)
    causal, _ = _seq_masks(nb, L)
    fcum = _cumsum_rows(lf, nb, L)
    bcol = z - fcum
    brow = bcol.T
    if L == SUBLANES:
        m_rows = m_ref[...]
    else:
        m_rows = jnp.concatenate(
            [jnp.broadcast_to(m_ref[s * SUBLANES:s * SUBLANES + 1, :], (L, LANES)) for s in range(nb)], axis=0)

    for h in range(M_HEADS):
        fc = fcum[:, h:h + 1]
        d = fc + brow[h:h + 1, :]
        rowmax = jnp.max(jnp.where(causal, d, -jnp.inf), axis=-1, keepdims=True)
        inter = m_rows[:, h:h + 1] + fc
        mt = jnp.maximum(rowmax, inter)
        p = jnp.where(causal, jnp.exp(jnp.where(causal, d - mt, 0.0)), 0.0)
        qh = q_ref[:, h * M_DK:(h + 1) * M_DK] * (M_DK ** -0.5)
        s = _dot_nt(qh, k_ref[:, h * M_DK:(h + 1) * M_DK]) * p
        num_s[:, h * M_DV:(h + 1) * M_DV] = _dot(s, v_ref[:, h * M_DV:(h + 1) * M_DV])
        for idx, val in enumerate((mt, jnp.exp(inter - mt), jnp.sum(s, axis=-1, keepdims=True),
                                   bcol[:, h:h + 1], fc)):
            col_s[idx, h] = jnp.broadcast_to(val, (R, LANES))

    def per_seq(s):
        rows = _rows(s, L)
        mrow = pl.ds(s * SUBLANES, 1)
        for h in range(M_HEADS):
            dk = slice(h * M_DK, (h + 1) * M_DK)
            dv = slice(h * M_DV, (h + 1) * M_DV)
            mt = col_s[0, h, rows, 0:1]
            w_inter = col_s[1, h, rows, 0:1]
            den_in = col_s[2, h, rows, 0:1]
            bc = col_s[3, h, rows, 0:1]
            f_last = col_s[4, h, rows, 0:1][L - 1:L]
            qh = q_ref[rows, dk] * (M_DK ** -0.5)
            kh = k_ref[rows, dk]
            vh = v_ref[rows, dv]
            c_old = c_ref[s, h]
            n_old = n_ref[s, h:h + 1, :]
            m_old = m_ref[mrow, h:h + 1]
            num = w_inter * _dot(qh, c_old) + num_s[rows, dv]
            den = w_inter * jnp.sum(qh * n_old, axis=-1, keepdims=True) + den_in
            hh = num / jnp.maximum(jnp.abs(den), jnp.exp(-mt))
            hn = hh * lax.rsqrt(jnp.mean(hh * hh, axis=-1, keepdims=True) + NORM_EPS) * nw_ref[:, dv]
            y_ref[rows, dv] = (hn * jax.nn.sigmoid(o_ref[rows, dv])).astype(y_ref.dtype)
            m_new = mt[L - 1:L]
            kw = kh * jnp.exp(bc + (f_last - m_new))
            scale = jnp.exp(m_old + f_last - m_new)
            c_ref[s, h] = scale * c_old + _dot_tn(kw, vh)
            n_ref[s, h:h + 1, :] = scale * n_old + jnp.sum(kw, axis=0, keepdims=True)
            m_ref[pl.ds(s * SUBLANES, SUBLANES), h:h + 1] = jnp.broadcast_to(m_new, (SUBLANES, 1))

    _for_each_seq(nb, per_seq)


def _mlstm_group(pm, pg, row0, nseq, nb, L, nch, c0, n0, m0, bias_row, norm_row):
    R = nb * L
    base = row0 // R
    nsb = nseq // nb
    rb = lambda sb, c: base + sb * nch + c
    full = lambda shape: pl.BlockSpec(shape, lambda sb, c: (0,) * len(shape))
    cst = pl.BlockSpec((nb, M_HEADS, M_DK, M_DV), lambda sb, c: (sb, 0, 0, 0))
    nst = pl.BlockSpec((nb, M_HEADS, M_DK), lambda sb, c: (sb, 0, 0))
    mst = pl.BlockSpec((nb * SUBLANES, LANES), lambda sb, c: (sb, 0))
    return pl.pallas_call(
        functools.partial(_mlstm_kernel, nb, L),
        grid=(nsb, nch),
        in_specs=[pl.BlockSpec((R, M_HEADS * M_DK), lambda sb, c: (rb(sb, c), COL_MQ // (M_HEADS * M_DK))),
                  pl.BlockSpec((R, M_HEADS * M_DK), lambda sb, c: (rb(sb, c), COL_MK // (M_HEADS * M_DK))),
                  pl.BlockSpec((R, M_WIDTH), lambda sb, c: (rb(sb, c), COL_MV // M_WIDTH)),
                  pl.BlockSpec((R, M_WIDTH), lambda sb, c: (rb(sb, c), COL_MO // M_WIDTH)),
                  pl.BlockSpec((R, LANES), lambda sb, c: (rb(sb, c), 0)),
                  full((1, LANES)), full((1, M_WIDTH)), cst, nst, mst],
        out_specs=[pl.BlockSpec((R, M_WIDTH), lambda sb, c: (sb * nch + c, 0)), cst, nst, mst],
        out_shape=[jax.ShapeDtypeStruct((nseq * nch * L, M_WIDTH), BF16),
                   jax.ShapeDtypeStruct((nseq, M_HEADS, M_DK, M_DV), F32),
                   jax.ShapeDtypeStruct((nseq, M_HEADS, M_DK), F32),
                   jax.ShapeDtypeStruct((nseq * SUBLANES, LANES), F32)],
        scratch_shapes=[pltpu.VMEM((R, M_WIDTH), F32), pltpu.VMEM((5, M_HEADS, R, LANES), F32)],
        compiler_params=_params(("parallel", "arbitrary")),
        name="mlstm",
    )(pm, pm, pm, pm, pg, bias_row, norm_row, c0, n0, m0)


GDN_HEAD_GROUP = 16


def _gdn_kernel(nb, L, q_ref, k_ref, v_ref, z_ref, g_ref, cw_ref, hist0_ref, alog_ref, dtb_ref, nw_ref,
                s0_ref, y_ref, s_ref, hist_ref, k_s, kq_s, vb_s, t_s, at_s, ws_s, vn_s, g_s):
    R = nb * L

    @pl.when(pl.program_id(1) == 0)
    def _():
        s_ref[...] = s0_ref[...]
        hist_ref[...] = hist0_ref[...]

    zt = g_ref[...]
    beta = pltpu.roll(jax.nn.sigmoid(zt), LANES - GATE_GB, 1)
    gdec = pltpu.roll(-jnp.exp(alog_ref[...]) * _softplus(zt + dtb_ref[...]), LANES - GATE_GA, 1)
    causal, strict = _seq_masks(nb, L)
    gcum = _cumsum_rows(gdec, nb, L)
    g_s[...] = gcum
    grow = gcum.T

    eye = (lax.broadcasted_iota(jnp.int32, (R, R), 0) == lax.broadcasted_iota(jnp.int32, (R, R), 1)).astype(F32)
    levels = int(math.log2(L)) - 1
    for h0 in range(0, G_HEADS, GDN_HEAD_GROUP):
        heads = range(h0, h0 + GDN_HEAD_GROUP)
        lhs, ks, decays = [], [], []
        for h in heads:
            sl = slice(h * G_DK, (h + 1) * G_DK)
            parts = []
            for idx, ref in enumerate((q_ref, k_ref, v_ref)):
                cs = slice(idx * G_WIDTH + h * G_DK, idx * G_WIDTH + (h + 1) * G_DK)
                parts.append(_silu(_conv4(ref[:, sl], hist_ref[:, cs], cw_ref[:, cs], nb, L)))
            qh, kh, vh = parts
            q = qh * lax.rsqrt(jnp.sum(qh * qh, axis=-1, keepdims=True) + NORM_EPS) * (G_DK ** -0.5)
            k = kh * lax.rsqrt(jnp.sum(kh * kh, axis=-1, keepdims=True) + NORM_EPS)
            bc = beta[:, h:h + 1]
            gc = gcum[:, h:h + 1]
            decays.append(jnp.where(causal, jnp.exp(jnp.where(causal, gc - grow[h:h + 1, :], 0.0)), 0.0))
            kb = k * bc
            eg = jnp.exp(gc)
            k_s[h] = k
            kq_s[h, 0] = kb * eg
            kq_s[h, 1] = q * eg
            vb_s[h] = vh * bc
            lhs.append(jnp.concatenate([kb, q], axis=0))
            ks.append(k)
        aas = [_dot_nt(l, k) for l, k in zip(lhs, ks)]
        avs = [jnp.where(strict, aa[:R] * d, 0.0) for aa, d in zip(aas, decays)]
        for h, aa, d in zip(heads, aas, decays):
            at_s[h] = aa[R:] * d
        ps = [eye - a for a in avs]
        bs = [_dot(a, a) for a in avs]
        for lvl in range(levels):
            if lvl + 1 < levels:
                xs = [_dot(b, jnp.concatenate([b, p], axis=1)) for b, p in zip(bs, ps)]
                bs = [x[:, :R] for x in xs]
                ps = [p + x[:, R:] for p, x in zip(ps, xs)]
            else:
                ps = [p + _dot(b, p) for b, p in zip(bs, ps)]
        for h, p in zip(heads, ps):
            t_s[h] = p

    def read_state(s):
        rows = _rows(s, L)
        for h in range(G_HEADS):
            x = _dot(jnp.concatenate([kq_s[h, 0, rows, :], kq_s[h, 1, rows, :]], axis=0), s_ref[s, h])
            ws_s[h, 0, rows, :] = x[:L]
            ws_s[h, 1, rows, :] = x[L:]

    _for_each_seq(nb, read_state)

    for h in range(G_HEADS):
        sl = slice(h * G_DK, (h + 1) * G_DK)
        v_new = _dot(t_s[h], vb_s[h] - ws_s[h, 0])
        vn_s[h] = v_new
        o = ws_s[h, 1] + _dot(at_s[h], v_new)
        on = o * lax.rsqrt(jnp.mean(o * o, axis=-1, keepdims=True) + NORM_EPS) * nw_ref[...]
        y_ref[:, sl] = (on * _silu(z_ref[:, sl])).astype(y_ref.dtype)

    def write_state(s):
        rows = _rows(s, L)
        for h in range(G_HEADS):
            gseq = g_s[rows, h:h + 1]
            g_end = gseq[L - 1:L]
            kdec = k_s[h, rows, :] * jnp.exp(g_end - gseq)
            s_ref[s, h] = s_ref[s, h] * jnp.exp(g_end) + _dot_tn(kdec, vn_s[h, rows, :])

    _for_each_seq(nb, write_state)

    for idx, ref in enumerate((q_ref, k_ref, v_ref)):
        hist_ref[:, idx * G_WIDTH:(idx + 1) * G_WIDTH] = _next_hist(ref[...], nb, L)


def _gdn_group(pm, pg, row0, nseq, nb, L, nch, s0, hist0, prm):
    R = nb * L
    base = row0 // R
    nsb = nseq // nb
    rb = lambda sb, c: base + sb * nch + c
    full = lambda shape: pl.BlockSpec(shape, lambda sb, c: (0,) * len(shape))
    sst = pl.BlockSpec((nb, G_HEADS, G_DK, G_DV), lambda sb, c: (sb, 0, 0, 0))
    hst = pl.BlockSpec((nb * SUBLANES, G_QKV), lambda sb, c: (sb, 0))
    col = lambda off: pl.BlockSpec((R, G_WIDTH), lambda sb, c: (rb(sb, c), off // G_WIDTH))
    return pl.pallas_call(
        functools.partial(_gdn_kernel, nb, L),
        grid=(nsb, nch),
        in_specs=[col(COL_GQ), col(COL_GK), col(COL_GV), col(COL_GZ),
                  pl.BlockSpec((R, LANES), lambda sb, c: (rb(sb, c), 0)),
                  full((CONV_W, G_QKV)), hst, full((1, LANES)), full((1, LANES)), full((1, G_DV)), sst],
        out_specs=[pl.BlockSpec((R, G_WIDTH), lambda sb, c: (sb * nch + c, 0)), sst, hst],
        out_shape=[jax.ShapeDtypeStruct((nseq * nch * L, G_WIDTH), BF16),
                   jax.ShapeDtypeStruct((nseq, G_HEADS, G_DK, G_DV), F32),
                   jax.ShapeDtypeStruct((nseq * SUBLANES, G_QKV), F32)],
        scratch_shapes=[pltpu.VMEM((G_HEADS, R, G_DK), F32), pltpu.VMEM((G_HEADS, 2, R, G_DK), F32),
                        pltpu.VMEM((G_HEADS, R, G_DV), F32), pltpu.VMEM((G_HEADS, R, R), F32),
                        pltpu.VMEM((G_HEADS, R, R), F32), pltpu.VMEM((G_HEADS, 2, R, G_DV), F32),
                        pltpu.VMEM((G_HEADS, R, G_DV), F32), pltpu.VMEM((R, LANES), F32)],
        compiler_params=_params(("parallel", "arbitrary")),
        name="gdn",
    )(pm, pm, pm, pm, pg, prm[0], hist0, prm[1], prm[2], prm[3], s0)


def _lane_row(vec, offset):
    return jnp.zeros((1, LANES), F32).at[0, offset:offset + vec.shape[0]].set(vec.astype(F32))


def _pad_hist(buf):
    n, w, c = buf.shape
    return jnp.pad(buf.astype(F32), ((0, 0), (0, SUBLANES - w), (0, 0))).reshape(n * SUBLANES, c)


def _unpad_hist(hist, n):
    return hist.reshape(n, SUBLANES, hist.shape[-1])[:, :CONV_W - 1]


def _mixers(pm, pg, group, states, lw):
    row0, nseq, nb_m, nb_r, nb_g, L_m, L_r, L_g, T = group
    c0, n0, m0, h0, rhist, s0, ghist = states
    ym, c, n, m = _mlstm_group(pm, pg, row0, nseq, nb_m, L_m, T // L_m, c0, n0, m0, lw["m_bias"], lw["m_norm"])
    yr, h, rh = _rglru_group(pm, row0, nseq, nb_r, L_r, T // L_r, rhist, h0, lw["r_prm"])
    yg, s, gh = _gdn_group(pm, pg, row0, nseq, nb_g, L_g, T // L_g, s0, ghist, lw["g_prm"])
    return (ym, yr, yg), (c, n, m, h, rh, s, gh)


def kernel(x_prompt, x_sample, state_mlstm_C, state_mlstm_n, state_mlstm_m, state_rglru_h, state_rglru_conv, state_gdn_S, state_gdn_conv, meta_tokens, norm_mix, w_in, m_bias_i, m_bias_f, m_norm, r_conv_w, r_conv_b, r_gate_a_w, r_gate_a_b, r_gate_x_w, r_gate_x_b, r_lambda, g_conv_w, g_A_log, g_dt_bias, g_norm, w_out, norm_ffn, w_gate, w_up, w_down, norm_final):
    batch, seq, d = x_prompt.shape
    dec_batch, dec_seq, _ = x_sample.shape
    depth = w_in.shape[0]
    n_prompt = batch * seq
    n_sample = dec_batch * dec_seq
    n_rows = n_prompt + n_sample + batch * N_META

    meta = jnp.broadcast_to(meta_tokens.astype(F32)[None], (batch, N_META, d))
    x = jnp.concatenate([x_prompt.reshape(n_prompt, d), x_sample.reshape(n_sample, d),
                         meta.reshape(batch * N_META, d)], axis=0)

    g_meta = (n_prompt + n_sample, batch, batch, batch, batch, N_META, N_META, N_META, N_META)
    g_prompt = (0, batch, 1, 1, 1, 256, 256, 64, seq)
    g_sample = (n_prompt, dec_batch, 16, 32, 8, dec_seq, dec_seq, dec_seq, dec_seq)

    zeros = lambda *shape: jnp.zeros(shape, F32)
    meta_states = (zeros(batch, M_HEADS, M_DK, M_DV), zeros(batch, M_HEADS, M_DK), zeros(batch * SUBLANES, LANES),
                   zeros(batch * SUBLANES, R_WIDTH), zeros(batch * SUBLANES, R_WIDTH),
                   zeros(batch, G_HEADS, G_DK, G_DV), zeros(batch * SUBLANES, G_QKV))

    prompt_new, sample_new = [], []
    for l in range(depth):
        w = w_in[l]
        w_main = jnp.concatenate([w[:, _SRC["gq"]:_SRC["gb"]], w[:, _SRC["mv"]:_SRC["mi"]],
                                  w[:, _SRC["rx"]:_SRC["gq"]], w[:, _SRC["mq"]:_SRC["mv"]]], axis=1).astype(BF16)
        w_gates = jnp.concatenate([w[:, _SRC["mi"]:_SRC["rx"]], w[:, _SRC["gb"]:_SRC["end"]],
                                   jnp.zeros((d, LANES - 40), w.dtype)], axis=1).astype(BF16)
        lw = dict(
            m_bias=_lane_row(m_bias_i[l], GATE_MI) + _lane_row(m_bias_f[l], GATE_MF),
            m_norm=m_norm[l].reshape(1, M_WIDTH),
            r_prm=(r_conv_w[l], r_conv_b[l].reshape(1, R_WIDTH), r_gate_a_w[l].astype(BF16),
                   r_gate_a_b[l].reshape(1, R_WIDTH), r_gate_x_w[l].astype(BF16),
                   r_gate_x_b[l].reshape(1, R_WIDTH), r_lambda[l].reshape(1, R_WIDTH)),
            g_prm=(g_conv_w[l], _lane_row(g_A_log[l], GATE_GA), _lane_row(g_dt_bias[l], GATE_GA),
                   g_norm[l].reshape(1, G_DV)),
        )
        sample_states = (state_mlstm_C[l], state_mlstm_n[l],
                         jnp.pad(jnp.repeat(state_mlstm_m[l], SUBLANES, axis=0), ((0, 0), (0, LANES - M_HEADS))),
                         jnp.repeat(state_rglru_h[l], SUBLANES, axis=0), _pad_hist(state_rglru_conv[l]),
                         state_gdn_S[l], _pad_hist(state_gdn_conv[l]))

        hn = _rmsnorm(x, norm_mix[l], BF16)
        pm = _matmul(hn, w_main, 1856, 512, "in_proj")
        pg = _matmul(hn, w_gates, 1856, LANES, "in_proj_gates")

        y_meta, st_meta = _mixers(pm, pg, g_meta, meta_states, lw)
        y_prompt, st_prompt = _mixers(pm, pg, g_prompt, st_meta, lw)
        y_sample, st_sample = _mixers(pm, pg, g_sample, sample_states, lw)
        mix = [jnp.concatenate([y_prompt[i], y_sample[i], y_meta[i]], axis=0) for i in range(3)]
        prompt_new.append(st_prompt)
        sample_new.append(st_sample)

        x = _matmul_resid(mix, w_out[l].astype(BF16), x, 1856, 256, "out_proj")
        hf = _rmsnorm(x, norm_ffn[l], BF16)
        hmid = _matmul_swiglu(hf, w_gate[l].astype(BF16), w_up[l].astype(BF16), 1856, 256)
        x = _matmul_resid([hmid], w_down[l].astype(BF16), x, 464, 512, "ffn_down")

    y = _rmsnorm(x, norm_final, F32)
    y_prompt_out = y[:n_prompt].reshape(batch, seq, d)
    y_sample_out = y[n_prompt:n_prompt + n_sample].reshape(dec_batch, dec_seq, d)

    def unpack(per_layer, n):
        c, nn, m, h, rh, s, gh = (jnp.stack([st[i] for st in per_layer]) for i in range(7))
        return (c, nn, m[:, ::SUBLANES, :M_HEADS], h[:, ::SUBLANES],
                jax.vmap(lambda a: _unpad_hist(a, n))(rh), s, jax.vmap(lambda a: _unpad_hist(a, n))(gh))

    return (y_prompt_out, y_sample_out) + unpack(prompt_new, batch) + unpack(sample_new, dec_batch)
```

```python
import functools
import math

import jax
import jax.numpy as jnp
from jax import lax
from jax.experimental import pallas as pl
from jax.experimental.pallas import tpu as pltpu

F32 = jnp.float32
BF16 = jnp.bfloat16
HIGHEST = lax.Precision.HIGHEST

D_MODEL = 4096
N_META = 16
CONV_W = 4
NORM_EPS = 1e-6
M_HEADS = 4
M_DV = 256
M_DK = 128
M_WIDTH = M_HEADS * M_DV
R_WIDTH = 1024
R_BLOCKS = 8
R_BDIM = R_WIDTH // R_BLOCKS
R_C = 8.0
G_DK = 128
G_DV = 128
G_HEADS = 16
G_WIDTH = G_HEADS * G_DV
G_QKV = 3 * G_WIDTH
D_FF = 11008

_SRC = dict(mq=0, mk=512, mv=1024, mo=2048, mi=3072, mf=3076, rx=3080, rg=4104,
            gq=5128, gk=7176, gv=9224, gz=11272, gb=13320, ga=13336, end=13352)
COL_GQ, COL_GK, COL_GV, COL_GZ = 0, 2048, 4096, 6144
COL_MV, COL_MO, COL_RX, COL_RG, COL_MQ, COL_MK = 8192, 9216, 10240, 11264, 12288, 12800
N_MAIN = 13312
LANES = 128
GATE_MI, GATE_MF, GATE_GB, GATE_GA = 0, 4, 8, 24

SUBLANES = 8
VMEM_LIMIT = 56 * 2 ** 20


def _params(semantics):
    return pltpu.CompilerParams(dimension_semantics=semantics, vmem_limit_bytes=VMEM_LIMIT)


def _dot(a, b):
    return jnp.dot(a.astype(BF16), b.astype(BF16), preferred_element_type=F32)


def _dot_nt(a, b):
    return lax.dot_general(a.astype(BF16), b.astype(BF16), (((1,), (1,)), ((), ())),
                           preferred_element_type=F32)


def _dot_tn(a, b):
    return lax.dot_general(a.astype(BF16), b.astype(BF16), (((0,), (0,)), ((), ())),
                           preferred_element_type=F32)


def _dot_f32(a, b):
    return jnp.dot(a, b, precision=HIGHEST, preferred_element_type=F32)


def _softplus(x):
    return jnp.maximum(x, 0.0) + jnp.log1p(jnp.exp(-jnp.abs(x)))


def _silu(x):
    return x * jax.nn.sigmoid(x)


def _seq_masks(nb, L):
    R = nb * L
    row = lax.broadcasted_iota(jnp.int32, (R, R), 0)
    col = lax.broadcasted_iota(jnp.int32, (R, R), 1)
    causal = row >= col
    strict = row > col
    if nb > 1:
        same = (row // L) == (col // L)
        causal = jnp.logical_and(causal, same)
        strict = jnp.logical_and(strict, same)
    return causal, strict


def _cumsum_rows(x, nb, L):
    tpos = lax.broadcasted_iota(jnp.int32, (nb * L, 1), 0) % L
    step = 1
    while step < L:
        x = x + jnp.where(tpos >= step, pltpu.roll(x, step, 0), 0.0)
        step *= 2
    return x


def _conv4(x, hist, w, nb, L):
    R = nb * L
    acc = x * w[CONV_W - 1:CONV_W, :]
    if L == SUBLANES:
        tpos = lax.broadcasted_iota(jnp.int32, (R, 1), 0) % L
    else:
        tpos = lax.broadcasted_iota(jnp.int32, (SUBLANES, 1), 0)
    for j in range(1, CONV_W):
        xs = pltpu.roll(x, j, 0)
        shift = (j - (CONV_W - 1)) % (nb * SUBLANES)
        hr = hist if shift == 0 else pltpu.roll(hist, shift, 0)
        if L == SUBLANES:
            sh = jnp.where(tpos < j, hr, xs)
        else:
            parts = []
            for s in range(nb):
                head = jnp.where(tpos < j, hr[s * SUBLANES:(s + 1) * SUBLANES],
                                 xs[s * L:s * L + SUBLANES])
                parts += [head, xs[s * L + SUBLANES:(s + 1) * L]]
            sh = jnp.concatenate(parts, axis=0)
        acc = acc + sh * w[CONV_W - 1 - j:CONV_W - j, :]
    return acc


def _next_hist(x, nb, L):
    if L == SUBLANES:
        return pltpu.roll(x, nb * L - (SUBLANES - (CONV_W - 1)), 0)
    parts = [pltpu.roll(x[(s + 1) * L - SUBLANES:(s + 1) * L], CONV_W - 1, 0) for s in range(nb)]
    return parts[0] if nb == 1 else jnp.concatenate(parts, axis=0)


def _for_each_seq(nb, fn):
    if nb <= 4:
        for s in range(nb):
            fn(s)
    else:
        def body(s, carry):
            fn(s)
            return carry
        lax.fori_loop(0, nb, body, 0)


def _rows(s, L):
    start = s * L
    if not isinstance(start, int):
        start = pl.multiple_of(start, SUBLANES)
    return pl.ds(start, L)


def _rmsnorm_kernel(x_ref, w_ref, o_ref):
    x = x_ref[...]
    y = x * lax.rsqrt(jnp.mean(x * x, axis=-1, keepdims=True) + NORM_EPS)
    o_ref[...] = (y * w_ref[...]).astype(o_ref.dtype)


def _rmsnorm(x, w, out_dtype, tm=464):
    n, d = x.shape
    return pl.pallas_call(
        _rmsnorm_kernel,
        grid=(n // tm,),
        in_specs=[pl.BlockSpec((tm, d), lambda i: (i, 0)), pl.BlockSpec((1, d), lambda i: (0, 0))],
        out_specs=pl.BlockSpec((tm, d), lambda i: (i, 0)),
        out_shape=jax.ShapeDtypeStruct((n, d), out_dtype),
        compiler_params=_params(("parallel",)),
        name="rmsnorm",
    )(x, w.reshape(1, d))


def _final_norm_kernel(n_first, x_ref, w_ref, a_ref, b_ref):
    x = x_ref[...]
    y = x * lax.rsqrt(jnp.mean(x * x, axis=-1, keepdims=True) + NORM_EPS) * w_ref[...]
    i = pl.program_id(0)

    @pl.when(i < n_first)
    def _():
        a_ref[...] = y

    @pl.when(i >= n_first)
    def _():
        b_ref[...] = y


def _final_norm(x, w, rows_a, rows_b, tm=512):
    d = x.shape[1]
    na, nb = rows_a // tm, rows_b // tm
    return pl.pallas_call(
        functools.partial(_final_norm_kernel, na),
        grid=(na + nb,),
        in_specs=[pl.BlockSpec((tm, d), lambda i: (i, 0)), pl.BlockSpec((1, d), lambda i: (0, 0))],
        out_specs=[pl.BlockSpec((tm, d), lambda i: (jnp.minimum(i, na - 1), 0)),
                   pl.BlockSpec((tm, d), lambda i: (jnp.maximum(i - na, 0), 0))],
        out_shape=[jax.ShapeDtypeStruct((rows_a, d), F32), jax.ShapeDtypeStruct((rows_b, d), F32)],
        compiler_params=_params(("arbitrary",)),
        name="final_norm",
    )(x, w.reshape(1, d))


def _mm_kernel(a_ref, w_ref, o_ref):
    o_ref[...] = jnp.dot(a_ref[...], w_ref[...], preferred_element_type=F32).astype(o_ref.dtype)


def _matmul(a, w, tm, tn, name):
    m, k = a.shape
    n = w.shape[1]
    return pl.pallas_call(
        _mm_kernel,
        grid=(m // tm, n // tn),
        in_specs=[pl.BlockSpec((tm, k), lambda i, j: (i, 0)), pl.BlockSpec((k, tn), lambda i, j: (0, j))],
        out_specs=pl.BlockSpec((tm, tn), lambda i, j: (i, j)),
        out_shape=jax.ShapeDtypeStruct((m, n), F32),
        compiler_params=_params(("parallel", "arbitrary")),
        name=name,
    )(a, w)


def _mm_swiglu_kernel(a_ref, wg_ref, wu_ref, o_ref):
    a = a_ref[...]
    g = jnp.dot(a, wg_ref[...].astype(BF16), preferred_element_type=F32)
    u = jnp.dot(a, wu_ref[...].astype(BF16), preferred_element_type=F32)
    o_ref[...] = (_silu(g) * u).astype(o_ref.dtype)


def _matmul_swiglu(a, wg, wu, tm, tn):
    m, k = a.shape
    n = wg.shape[1]
    return pl.pallas_call(
        _mm_swiglu_kernel,
        grid=(m // tm, n // tn),
        in_specs=[pl.BlockSpec((tm, k), lambda i, j: (i, 0)),
                  pl.BlockSpec((k, tn), lambda i, j: (0, j)),
                  pl.BlockSpec((k, tn), lambda i, j: (0, j))],
        out_specs=pl.BlockSpec((tm, tn), lambda i, j: (i, j)),
        out_shape=jax.ShapeDtypeStruct((m, n), BF16),
        compiler_params=_params(("parallel", "arbitrary")),
        name="ffn_gate_up",
    )(a, wg, wu)


def _mm_resid_kernel(widths, *refs):
    a_refs = refs[:len(widths)]
    w_ref, x_ref, o_ref = refs[len(widths):]
    acc = x_ref[...]
    k0 = 0
    for a_ref, kw in zip(a_refs, widths):
        acc = acc + jnp.dot(a_ref[...], w_ref[k0:k0 + kw, :].astype(BF16), preferred_element_type=F32)
        k0 += kw
    o_ref[...] = acc


def _matmul_resid(a_list, w, x, tm, tn, name):
    m = x.shape[0]
    k, n = w.shape
    widths = tuple(a.shape[1] for a in a_list)
    return pl.pallas_call(
        functools.partial(_mm_resid_kernel, widths),
        grid=(m // tm, n // tn),
        in_specs=[pl.BlockSpec((tm, kw), lambda i, j: (i, 0)) for kw in widths]
        + [pl.BlockSpec((k, tn), lambda i, j: (0, j)), pl.BlockSpec((tm, tn), lambda i, j: (i, j))],
        out_specs=pl.BlockSpec((tm, tn), lambda i, j: (i, j)),
        out_shape=jax.ShapeDtypeStruct((m, n), F32),
        compiler_params=_params(("parallel", "arbitrary")),
        name=name,
    )(*a_list, w, x)


def _rglru_kernel(nb, L, _y_all_ref, rx_ref, rg_ref, hist0_ref, h0_ref, cw_ref, cb_ref, wa_ref, ba_ref, wx_ref,
                  bx_ref, lam_ref, y_ref, h_ref, hist_ref, a_s, u_s):
    @pl.when(pl.program_id(1) == 0)
    def _():
        h_ref[...] = h0_ref[...]
        hist_ref[...] = hist0_ref[...]

    x = rx_ref[...]
    xc = _conv4(x, hist_ref[...], cw_ref[...], nb, L) + cb_ref[...]
    hist_ref[...] = _next_hist(x, nb, L)

    xcb = xc.astype(BF16)
    ga = jnp.concatenate([jnp.dot(xcb[:, n * R_BDIM:(n + 1) * R_BDIM], wa_ref[n], preferred_element_type=F32)
                          for n in range(R_BLOCKS)], axis=1)
    gx = jnp.concatenate([jnp.dot(xcb[:, n * R_BDIM:(n + 1) * R_BDIM], wx_ref[n], preferred_element_type=F32)
                          for n in range(R_BLOCKS)], axis=1)
    r = jax.nn.sigmoid(ga + ba_ref[...])
    i = jax.nn.sigmoid(gx + bx_ref[...])
    log_a = -R_C * r * _softplus(-lam_ref[...])
    a_s[...] = jnp.exp(log_a)
    th = jnp.tanh(log_a)
    u_s[...] = jnp.sqrt(-2.0 * th / (1.0 - th)) * (i * xc)

    if nb == 1:
        def step(t, h):
            h = a_s[pl.ds(t, 1), :] * h + u_s[pl.ds(t, 1), :]
            u_s[pl.ds(t, 1), :] = h
            return h

        h_last = lax.fori_loop(0, L, step, h_ref[0:1, :], unroll=SUBLANES)
        h_ref[...] = jnp.broadcast_to(h_last, (SUBLANES, R_WIDTH))
        hs = u_s[...]
    else:
        a, u = a_s[...], u_s[...]
        tpos = lax.broadcasted_iota(jnp.int32, (nb * L, 1), 0) % L
        step = 1
        while step < L:
            valid = tpos >= step
            u = jnp.where(valid, a * pltpu.roll(u, step, 0) + u, u)
            a = jnp.where(valid, a * pltpu.roll(a, step, 0), a)
            step *= 2
        if L == SUBLANES:
            h0 = h_ref[...]
        else:
            h0 = jnp.concatenate([jnp.broadcast_to(h_ref[s * SUBLANES:s * SUBLANES + 1, :], (L, R_WIDTH))
                                  for s in range(nb)], axis=0)
        hs = a * h0 + u
        if L == SUBLANES:
            h_ref[...] = pltpu.roll(hs, nb * L - (L - 1), 0)
        else:
            h_ref[...] = jnp.concatenate([jnp.broadcast_to(hs[(s + 1) * L - 1:(s + 1) * L, :], (SUBLANES, R_WIDTH))
                                          for s in range(nb)], axis=0)
    y_ref[...] = (hs * jax.nn.gelu(rg_ref[...])).astype(y_ref.dtype)


def _rglru_group(pm, y_all, row0, nseq, nb, L, nch, hist0, h0, prm):
    R = nb * L
    base = row0 // R
    nsb = nseq // nb
    rb = lambda sb, c: base + sb * nch + c
    full = lambda shape: pl.BlockSpec(shape, lambda sb, c: (0,) * len(shape))
    st = pl.BlockSpec((nb * SUBLANES, R_WIDTH), lambda sb, c: (sb, 0))
    return pl.pallas_call(
        functools.partial(_rglru_kernel, nb, L),
        grid=(nsb, nch),
        in_specs=[pl.BlockSpec(memory_space=pl.ANY),
                  pl.BlockSpec((R, R_WIDTH), lambda sb, c: (rb(sb, c), COL_RX // R_WIDTH)),
                  pl.BlockSpec((R, R_WIDTH), lambda sb, c: (rb(sb, c), COL_RG // R_WIDTH)),
                  st, st,
                  full((CONV_W, R_WIDTH)), full((1, R_WIDTH)),
                  full((R_BLOCKS, R_BDIM, R_BDIM)), full((1, R_WIDTH)),
                  full((R_BLOCKS, R_BDIM, R_BDIM)), full((1, R_WIDTH)), full((1, R_WIDTH))],
        out_specs=[pl.BlockSpec((R, R_WIDTH), lambda sb, c: (rb(sb, c), 0)), st, st],
        out_shape=[jax.ShapeDtypeStruct(y_all.shape, y_all.dtype),
                   jax.ShapeDtypeStruct((nseq * SUBLANES, R_WIDTH), F32),
                   jax.ShapeDtypeStruct((nseq * SUBLANES, R_WIDTH), F32)],
        scratch_shapes=[pltpu.VMEM((R, R_WIDTH), F32), pltpu.VMEM((R, R_WIDTH), F32)],
        compiler_params=_params(("parallel", "arbitrary")),
        input_output_aliases={0: 0},
        name="rglru",
    )(y_all, pm, pm, hist0, h0, *prm)


def _mlstm_kernel(nb, L, _y_all_ref, q_ref, k_ref, v_ref, o_ref, g_ref, bias_ref, nw_ref, c0_ref, n0_ref, m0_ref,
                  y_ref, c_ref, n_ref, m_ref, num_s, col_s):
    R = nb * L

    @pl.when(pl.program_id(1) == 0)
    def _():
        c_ref[...] = c0_ref[...]
        n_ref[...] = n0_ref[...]
        m_ref[...] = m0_ref[...]

    z = g_ref[...] + bias_ref[...]
    lf = pltpu.roll(-_softplus(-z), LANES - GATE_MF, 1)
    causal, _ = _seq_masks(nb, L)
    fcum = _cumsum_rows(lf, nb, L)
    bcol = z - fcum
    brow = bcol.T
    if L == SUBLANES:
        m_rows = m_ref[...]
    else:
        m_rows = jnp.concatenate(
            [jnp.broadcast_to(m_ref[s * SUBLANES:s * SUBLANES + 1, :], (L, LANES)) for s in range(nb)], axis=0)

    for h in range(M_HEADS):
        fc = fcum[:, h:h + 1]
        d = fc + brow[h:h + 1, :]
        rowmax = jnp.max(jnp.where(causal, d, -jnp.inf), axis=-1, keepdims=True)
        inter = m_rows[:, h:h + 1] + fc
        mt = jnp.maximum(rowmax, inter)
        p = jnp.where(causal, jnp.exp(jnp.where(causal, d - mt, 0.0)), 0.0)
        qh = q_ref[:, h * M_DK:(h + 1) * M_DK] * (M_DK ** -0.5)
        s = _dot_nt(qh, k_ref[:, h * M_DK:(h + 1) * M_DK]) * p
        num_s[:, h * M_DV:(h + 1) * M_DV] = _dot(s, v_ref[:, h * M_DV:(h + 1) * M_DV])
        for idx, val in enumerate((mt, jnp.exp(inter - mt), jnp.sum(s, axis=-1, keepdims=True),
                                   bcol[:, h:h + 1], fc)):
            col_s[idx, h] = jnp.broadcast_to(val, (R, LANES))

    def per_seq(s):
        rows = _rows(s, L)
        mrow = pl.ds(s * SUBLANES, 1)
        for h in range(M_HEADS):
            dk = slice(h * M_DK, (h + 1) * M_DK)
            dv = slice(h * M_DV, (h + 1) * M_DV)
            mt = col_s[0, h, rows, 0:1]
            w_inter = col_s[1, h, rows, 0:1]
            den_in = col_s[2, h, rows, 0:1]
            bc = col_s[3, h, rows, 0:1]
            f_last = col_s[4, h, rows, 0:1][L - 1:L]
            qh = q_ref[rows, dk] * (M_DK ** -0.5)
            kh = k_ref[rows, dk]
            vh = v_ref[rows, dv]
            c_old = c_ref[s, h]
            n_old = n_ref[s, h:h + 1, :]
            m_old = m_ref[mrow, h:h + 1]
            num = w_inter * _dot(qh, c_old) + num_s[rows, dv]
            den = w_inter * jnp.sum(qh * n_old, axis=-1, keepdims=True) + den_in
            hh = num / jnp.maximum(jnp.abs(den), jnp.exp(-mt))
            hn = hh * lax.rsqrt(jnp.mean(hh * hh, axis=-1, keepdims=True) + NORM_EPS) * nw_ref[:, dv]
            y_ref[rows, dv] = (hn * jax.nn.sigmoid(o_ref[rows, dv])).astype(y_ref.dtype)
            m_new = mt[L - 1:L]
            kw = kh * jnp.exp(bc + (f_last - m_new))
            scale = jnp.exp(m_old + f_last - m_new)
            c_ref[s, h] = scale * c_old + _dot_tn(kw, vh)
            n_ref[s, h:h + 1, :] = scale * n_old + jnp.sum(kw, axis=0, keepdims=True)
            m_ref[pl.ds(s * SUBLANES, SUBLANES), h:h + 1] = jnp.broadcast_to(m_new, (SUBLANES, 1))

    _for_each_seq(nb, per_seq)


def _mlstm_group(pm, pg, y_all, row0, nseq, nb, L, nch, c0, n0, m0, bias_row, norm_row):
    R = nb * L
    base = row0 // R
    nsb = nseq // nb
    rb = lambda sb, c: base + sb * nch + c
    full = lambda shape: pl.BlockSpec(shape, lambda sb, c: (0,) * len(shape))
    cst = pl.BlockSpec((nb, M_HEADS, M_DK, M_DV), lambda sb, c: (sb, 0, 0, 0))
    nst = pl.BlockSpec((nb, M_HEADS, M_DK), lambda sb, c: (sb, 0, 0))
    mst = pl.BlockSpec((nb * SUBLANES, LANES), lambda sb, c: (sb, 0))
    return pl.pallas_call(
        functools.partial(_mlstm_kernel, nb, L),
        grid=(nsb, nch),
        in_specs=[pl.BlockSpec(memory_space=pl.ANY),
                  pl.BlockSpec((R, M_HEADS * M_DK), lambda sb, c: (rb(sb, c), COL_MQ // (M_HEADS * M_DK))),
                  pl.BlockSpec((R, M_HEADS * M_DK), lambda sb, c: (rb(sb, c), COL_MK // (M_HEADS * M_DK))),
                  pl.BlockSpec((R, M_WIDTH), lambda sb, c: (rb(sb, c), COL_MV // M_WIDTH)),
                  pl.BlockSpec((R, M_WIDTH), lambda sb, c: (rb(sb, c), COL_MO // M_WIDTH)),
                  pl.BlockSpec((R, LANES), lambda sb, c: (rb(sb, c), 0)),
                  full((1, LANES)), full((1, M_WIDTH)), cst, nst, mst],
        out_specs=[pl.BlockSpec((R, M_WIDTH), lambda sb, c: (rb(sb, c), 0)), cst, nst, mst],
        out_shape=[jax.ShapeDtypeStruct(y_all.shape, y_all.dtype),
                   jax.ShapeDtypeStruct((nseq, M_HEADS, M_DK, M_DV), F32),
                   jax.ShapeDtypeStruct((nseq, M_HEADS, M_DK), F32),
                   jax.ShapeDtypeStruct((nseq * SUBLANES, LANES), F32)],
        scratch_shapes=[pltpu.VMEM((R, M_WIDTH), F32), pltpu.VMEM((5, M_HEADS, R, LANES), F32)],
        compiler_params=_params(("parallel", "arbitrary")),
        input_output_aliases={0: 0},
        name="mlstm",
    )(y_all, pm, pm, pm, pm, pg, bias_row, norm_row, c0, n0, m0)


GDN_HEAD_GROUP = 16


def _gdn_kernel(nb, L, _y_all_ref, q_ref, k_ref, v_ref, z_ref, g_ref, cw_ref, hist0_ref, alog_ref, dtb_ref, nw_ref,
                s0_ref, y_ref, s_ref, hist_ref, k_s, kq_s, vb_s, t_s, at_s, ws_s, vn_s, g_s):
    R = nb * L

    @pl.when(pl.program_id(1) == 0)
    def _():
        s_ref[...] = s0_ref[...]
        hist_ref[...] = hist0_ref[...]

    zt = g_ref[...]
    beta = pltpu.roll(jax.nn.sigmoid(zt), LANES - GATE_GB, 1)
    gdec = pltpu.roll(-jnp.exp(alog_ref[...]) * _softplus(zt + dtb_ref[...]), LANES - GATE_GA, 1)
    causal, strict = _seq_masks(nb, L)
    gcum = _cumsum_rows(gdec, nb, L)
    g_s[...] = gcum
    grow = gcum.T

    eye = (lax.broadcasted_iota(jnp.int32, (R, R), 0) == lax.broadcasted_iota(jnp.int32, (R, R), 1)).astype(F32)
    levels = int(math.log2(L)) - 1
    for h0 in range(0, G_HEADS, GDN_HEAD_GROUP):
        heads = range(h0, h0 + GDN_HEAD_GROUP)
        lhs, ks, decays = [], [], []
        for h in heads:
            sl = slice(h * G_DK, (h + 1) * G_DK)
            parts = []
            for idx, ref in enumerate((q_ref, k_ref, v_ref)):
                cs = slice(idx * G_WIDTH + h * G_DK, idx * G_WIDTH + (h + 1) * G_DK)
                parts.append(_silu(_conv4(ref[:, sl], hist_ref[:, cs], cw_ref[:, cs], nb, L)))
            qh, kh, vh = parts
            q = qh * lax.rsqrt(jnp.sum(qh * qh, axis=-1, keepdims=True) + NORM_EPS) * (G_DK ** -0.5)
            k = kh * lax.rsqrt(jnp.sum(kh * kh, axis=-1, keepdims=True) + NORM_EPS)
            bc = beta[:, h:h + 1]
            gc = gcum[:, h:h + 1]
            decays.append(jnp.where(causal, jnp.exp(jnp.where(causal, gc - grow[h:h + 1, :], 0.0)), 0.0))
            kb = k * bc
            eg = jnp.exp(gc)
            k_s[h] = k
            kq_s[h, 0] = kb * eg
            kq_s[h, 1] = q * eg
            vb_s[h] = vh * bc
            lhs.append(jnp.concatenate([kb, q], axis=0))
            ks.append(k)
        aas = [_dot_nt(l, k) for l, k in zip(lhs, ks)]
        avs = [jnp.where(strict, aa[:R] * d, 0.0) for aa, d in zip(aas, decays)]
        for h, aa, d in zip(heads, aas, decays):
            at_s[h] = aa[R:] * d
        ps = [eye - a for a in avs]
        bs = [_dot(a, a) for a in avs]
        for lvl in range(levels):
            if lvl + 1 < levels:
                xs = [_dot(b, jnp.concatenate([b, p], axis=1)) for b, p in zip(bs, ps)]
                bs = [x[:, :R] for x in xs]
                ps = [p + x[:, R:] for p, x in zip(ps, xs)]
            else:
                ps = [p + _dot(b, p) for b, p in zip(bs, ps)]
        for h, p in zip(heads, ps):
            t_s[h] = p

    def read_state(s):
        rows = _rows(s, L)
        for h in range(G_HEADS):
            x = _dot(jnp.concatenate([kq_s[h, 0, rows, :], kq_s[h, 1, rows, :]], axis=0), s_ref[s, h])
            ws_s[h, 0, rows, :] = x[:L]
            ws_s[h, 1, rows, :] = x[L:]

    _for_each_seq(nb, read_state)

    for h in range(G_HEADS):
        sl = slice(h * G_DK, (h + 1) * G_DK)
        v_new = _dot(t_s[h], vb_s[h] - ws_s[h, 0])
        vn_s[h] = v_new
        o = ws_s[h, 1] + _dot(at_s[h], v_new)
        on = o * lax.rsqrt(jnp.mean(o * o, axis=-1, keepdims=True) + NORM_EPS) * nw_ref[...]
        y_ref[:, sl] = (on * _silu(z_ref[:, sl])).astype(y_ref.dtype)

    def write_state(s):
        rows = _rows(s, L)
        for h in range(G_HEADS):
            gseq = g_s[rows, h:h + 1]
            g_end = gseq[L - 1:L]
            kdec = k_s[h, rows, :] * jnp.exp(g_end - gseq)
            s_ref[s, h] = s_ref[s, h] * jnp.exp(g_end) + _dot_tn(kdec, vn_s[h, rows, :])

    _for_each_seq(nb, write_state)

    for idx, ref in enumerate((q_ref, k_ref, v_ref)):
        hist_ref[:, idx * G_WIDTH:(idx + 1) * G_WIDTH] = _next_hist(ref[...], nb, L)


def _gdn_group(pm, pg, y_all, row0, nseq, nb, L, nch, s0, hist0, prm):
    R = nb * L
    base = row0 // R
    nsb = nseq // nb
    rb = lambda sb, c: base + sb * nch + c
    full = lambda shape: pl.BlockSpec(shape, lambda sb, c: (0,) * len(shape))
    sst = pl.BlockSpec((nb, G_HEADS, G_DK, G_DV), lambda sb, c: (sb, 0, 0, 0))
    hst = pl.BlockSpec((nb * SUBLANES, G_QKV), lambda sb, c: (sb, 0))
    col = lambda off: pl.BlockSpec((R, G_WIDTH), lambda sb, c: (rb(sb, c), off // G_WIDTH))
    return pl.pallas_call(
        functools.partial(_gdn_kernel, nb, L),
        grid=(nsb, nch),
        in_specs=[pl.BlockSpec(memory_space=pl.ANY), col(COL_GQ), col(COL_GK), col(COL_GV), col(COL_GZ),
                  pl.BlockSpec((R, LANES), lambda sb, c: (rb(sb, c), 0)),
                  full((CONV_W, G_QKV)), hst, full((1, LANES)), full((1, LANES)), full((1, G_DV)), sst],
        out_specs=[pl.BlockSpec((R, G_WIDTH), lambda sb, c: (rb(sb, c), 0)), sst, hst],
        out_shape=[jax.ShapeDtypeStruct(y_all.shape, y_all.dtype),
                   jax.ShapeDtypeStruct((nseq, G_HEADS, G_DK, G_DV), F32),
                   jax.ShapeDtypeStruct((nseq * SUBLANES, G_QKV), F32)],
        scratch_shapes=[pltpu.VMEM((G_HEADS, R, G_DK), F32), pltpu.VMEM((G_HEADS, 2, R, G_DK), F32),
                        pltpu.VMEM((G_HEADS, R, G_DV), F32), pltpu.VMEM((G_HEADS, R, R), F32),
                        pltpu.VMEM((G_HEADS, R, R), F32), pltpu.VMEM((G_HEADS, 2, R, G_DV), F32),
                        pltpu.VMEM((G_HEADS, R, G_DV), F32), pltpu.VMEM((R, LANES), F32)],
        compiler_params=_params(("parallel", "arbitrary")),
        input_output_aliases={0: 0},
        name="gdn",
    )(y_all, pm, pm, pm, pm, pg, prm[0], hist0, prm[1], prm[2], prm[3], s0)


def _lane_row(vec, offset):
    return jnp.zeros((1, LANES), F32).at[0, offset:offset + vec.shape[0]].set(vec.astype(F32))


def _pad_hist(buf):
    n, w, c = buf.shape
    return jnp.pad(buf.astype(F32), ((0, 0), (0, SUBLANES - w), (0, 0))).reshape(n * SUBLANES, c)


def _unpad_hist(hist, n):
    return hist.reshape(n, SUBLANES, hist.shape[-1])[:, :CONV_W - 1]


def _mixers(pm, pg, mix, group, states, lw):
    row0, nseq, nb_m, nb_r, nb_g, L_m, L_r, L_g, T = group
    c0, n0, m0, h0, rhist, s0, ghist = states
    ym, c, n, m = _mlstm_group(pm, pg, mix[0], row0, nseq, nb_m, L_m, T // L_m, c0, n0, m0,
                               lw["m_bias"], lw["m_norm"])
    yr, h, rh = _rglru_group(pm, mix[1], row0, nseq, nb_r, L_r, T // L_r, rhist, h0, lw["r_prm"])
    yg, s, gh = _gdn_group(pm, pg, mix[2], row0, nseq, nb_g, L_g, T // L_g, s0, ghist, lw["g_prm"])
    return (ym, yr, yg), (c, n, m, h, rh, s, gh)


def kernel(x_prompt, x_sample, state_mlstm_C, state_mlstm_n, state_mlstm_m, state_rglru_h, state_rglru_conv, state_gdn_S, state_gdn_conv, meta_tokens, norm_mix, w_in, m_bias_i, m_bias_f, m_norm, r_conv_w, r_conv_b, r_gate_a_w, r_gate_a_b, r_gate_x_w, r_gate_x_b, r_lambda, g_conv_w, g_A_log, g_dt_bias, g_norm, w_out, norm_ffn, w_gate, w_up, w_down, norm_final):
    batch, seq, d = x_prompt.shape
    dec_batch, dec_seq, _ = x_sample.shape
    depth = w_in.shape[0]
    n_prompt = batch * seq
    n_sample = dec_batch * dec_seq
    n_rows = n_prompt + n_sample + batch * N_META

    meta = jnp.broadcast_to(meta_tokens.astype(F32)[None], (batch, N_META, d))
    x = jnp.concatenate([x_prompt.reshape(n_prompt, d), x_sample.reshape(n_sample, d),
                         meta.reshape(batch * N_META, d)], axis=0)

    g_meta = (n_prompt + n_sample, batch, batch, batch, batch, N_META, N_META, N_META, N_META)
    g_prompt = (0, batch, 1, 1, 1, 256, 256, 64, seq)
    g_sample = (n_prompt, dec_batch, 16, 32, 8, dec_seq, dec_seq, dec_seq, dec_seq)

    zeros = lambda *shape: jnp.zeros(shape, F32)
    meta_states = (zeros(batch, M_HEADS, M_DK, M_DV), zeros(batch, M_HEADS, M_DK), zeros(batch * SUBLANES, LANES),
                   zeros(batch * SUBLANES, R_WIDTH), zeros(batch * SUBLANES, R_WIDTH),
                   zeros(batch, G_HEADS, G_DK, G_DV), zeros(batch * SUBLANES, G_QKV))

    prompt_new, sample_new = [], []
    for l in range(depth):
        w = w_in[l]
        w_main = jnp.concatenate([w[:, _SRC["gq"]:_SRC["gb"]], w[:, _SRC["mv"]:_SRC["mi"]],
                                  w[:, _SRC["rx"]:_SRC["gq"]], w[:, _SRC["mq"]:_SRC["mv"]]], axis=1).astype(BF16)
        w_gates = jnp.concatenate([w[:, _SRC["mi"]:_SRC["rx"]], w[:, _SRC["gb"]:_SRC["end"]],
                                   jnp.zeros((d, LANES - 40), w.dtype)], axis=1).astype(BF16)
        lw = dict(
            m_bias=_lane_row(m_bias_i[l], GATE_MI) + _lane_row(m_bias_f[l], GATE_MF),
            m_norm=m_norm[l].reshape(1, M_WIDTH),
            r_prm=(r_conv_w[l], r_conv_b[l].reshape(1, R_WIDTH), r_gate_a_w[l].astype(BF16),
                   r_gate_a_b[l].reshape(1, R_WIDTH), r_gate_x_w[l].astype(BF16),
                   r_gate_x_b[l].reshape(1, R_WIDTH), r_lambda[l].reshape(1, R_WIDTH)),
            g_prm=(g_conv_w[l], _lane_row(g_A_log[l], GATE_GA), _lane_row(g_dt_bias[l], GATE_GA),
                   g_norm[l].reshape(1, G_DV)),
        )
        sample_states = (state_mlstm_C[l], state_mlstm_n[l],
                         jnp.pad(jnp.repeat(state_mlstm_m[l], SUBLANES, axis=0), ((0, 0), (0, LANES - M_HEADS))),
                         jnp.repeat(state_rglru_h[l], SUBLANES, axis=0), _pad_hist(state_rglru_conv[l]),
                         state_gdn_S[l], _pad_hist(state_gdn_conv[l]))

        hn = _rmsnorm(x, norm_mix[l], BF16)
        pm = _matmul(hn, w_main, 1856, 512, "in_proj")
        pg = _matmul(hn, w_gates, 1856, LANES, "in_proj_gates")

        mix = tuple(jnp.zeros((n_rows, width), BF16) for width in (M_WIDTH, R_WIDTH, G_WIDTH))
        mix, st_meta = _mixers(pm, pg, mix, g_meta, meta_states, lw)
        mix, st_prompt = _mixers(pm, pg, mix, g_prompt, st_meta, lw)
        mix, st_sample = _mixers(pm, pg, mix, g_sample, sample_states, lw)
        prompt_new.append(st_prompt)
        sample_new.append(st_sample)

        x = _matmul_resid(mix, w_out[l], x, 1856, 256, "out_proj")
        hf = _rmsnorm(x, norm_ffn[l], BF16)
        hmid = _matmul_swiglu(hf, w_gate[l], w_up[l], 1856, 256)
        x = _matmul_resid([hmid], w_down[l].astype(BF16), x, 464, 512, "ffn_down")

    y_prompt_out, y_sample_out = _final_norm(x, norm_final, n_prompt, n_sample)
    y_prompt_out = y_prompt_out.reshape(batch, seq, d)
    y_sample_out = y_sample_out.reshape(dec_batch, dec_seq, d)

    def unpack(per_layer, n):
        c, nn, m, h, rh, s, gh = (jnp.stack([st[i] for st in per_layer]) for i in range(7))
        return (c, nn, m[:, ::SUBLANES, :M_HEADS], h[:, ::SUBLANES],
                jax.vmap(lambda a: _unpad_hist(a, n))(rh), s, jax.vmap(lambda a: _unpad_hist(a, n))(gh))

    return (y_prompt_out, y_sample_out) + unpack(prompt_new, batch) + unpack(sample_new, dec_batch)
```

```python
import functools
import math

import jax
import jax.numpy as jnp
from jax import lax
from jax.experimental import pallas as pl
from jax.experimental.pallas import tpu as pltpu

F32 = jnp.float32
BF16 = jnp.bfloat16

N_META = 16
CONV_W = 4
NORM_EPS = 1e-6
M_HEADS = 4
M_DV = 256
M_DK = 128
M_WIDTH = M_HEADS * M_DV
R_WIDTH = 1024
R_BLOCKS = 8
R_BDIM = R_WIDTH // R_BLOCKS
R_C = 8.0
G_DK = 128
G_DV = 128
G_HEADS = 16
G_WIDTH = G_HEADS * G_DV
G_QKV = 3 * G_WIDTH

LANES = 128
SUBLANES = 8
VMEM_LIMIT = 56 * 2 ** 20

SRC_GATES_M = 3072
SRC_S = 3080
SRC_SHIFT = SRC_S % LANES
N_A = 3072
N_S = 10240
SRC_GATES_G = SRC_S + N_S - SRC_SHIFT
N_IN = SRC_S + N_S + 32
COL_MQ, COL_MK, COL_MV, COL_MO = 0, 512, 1024, 2048
COL_RX, COL_RG, COL_GQ, COL_GK, COL_GV, COL_GZ = 0, 1024, 2048, 4096, 6144, 8192
GATE_MI, GATE_MF, GATE_GB, GATE_GA = 0, 4, 8, 24
N_GATES = 40

ROWS_DENSE = 1856
ROWS_DOWN = 464
COLS_DENSE = 256
COLS_DOWN = 512
ROWS_NORM = 464
ROWS_FINAL_NORM = 512
IN_PROJ_K_CHUNK = 512


def _params(semantics):
    return pltpu.CompilerParams(dimension_semantics=semantics, vmem_limit_bytes=VMEM_LIMIT)


def _dot(a, b):
    return jnp.dot(a.astype(BF16), b.astype(BF16), preferred_element_type=F32)


def _dot_nt(a, b):
    return lax.dot_general(a.astype(BF16), b.astype(BF16), (((1,), (1,)), ((), ())),
                           preferred_element_type=F32)


def _dot_tn(a, b):
    return lax.dot_general(a.astype(BF16), b.astype(BF16), (((0,), (0,)), ((), ())),
                           preferred_element_type=F32)


def _softplus(x):
    return jnp.maximum(x, 0.0) + jnp.log1p(jnp.exp(-jnp.abs(x)))


def _silu(x):
    return x * jax.nn.sigmoid(x)


def _seq_masks(nb, L):
    R = nb * L
    row = lax.broadcasted_iota(jnp.int32, (R, R), 0)
    col = lax.broadcasted_iota(jnp.int32, (R, R), 1)
    causal = row >= col
    strict = row > col
    if nb > 1:
        same = (row // L) == (col // L)
        causal = jnp.logical_and(causal, same)
        strict = jnp.logical_and(strict, same)
    return causal, strict


def _cumsum_rows(x, nb, L):
    tpos = lax.broadcasted_iota(jnp.int32, (nb * L, 1), 0) % L
    step = 1
    while step < L:
        x = x + jnp.where(tpos >= step, pltpu.roll(x, step, 0), 0.0)
        step *= 2
    return x


def _conv4(x, hist, w, nb, L):
    R = nb * L
    acc = x * w[CONV_W - 1:CONV_W, :]
    if L == SUBLANES:
        tpos = lax.broadcasted_iota(jnp.int32, (R, 1), 0) % L
    else:
        tpos = lax.broadcasted_iota(jnp.int32, (SUBLANES, 1), 0)
    for j in range(1, CONV_W):
        xs = pltpu.roll(x, j, 0)
        shift = (j - (CONV_W - 1)) % (nb * SUBLANES)
        hr = hist if shift == 0 else pltpu.roll(hist, shift, 0)
        if L == SUBLANES:
            sh = jnp.where(tpos < j, hr, xs)
        else:
            parts = []
            for s in range(nb):
                head = jnp.where(tpos < j, hr[s * SUBLANES:(s + 1) * SUBLANES],
                                 xs[s * L:s * L + SUBLANES])
                parts += [head, xs[s * L + SUBLANES:(s + 1) * L]]
            sh = jnp.concatenate(parts, axis=0)
        acc = acc + sh * w[CONV_W - 1 - j:CONV_W - j, :]
    return acc


def _next_hist(x, nb, L):
    if L == SUBLANES:
        return pltpu.roll(x, nb * L - (SUBLANES - (CONV_W - 1)), 0)
    parts = [pltpu.roll(x[(s + 1) * L - SUBLANES:(s + 1) * L], CONV_W - 1, 0) for s in range(nb)]
    return parts[0] if nb == 1 else jnp.concatenate(parts, axis=0)


def _for_each_seq(nb, fn):
    if nb <= 4:
        for s in range(nb):
            fn(s)
    else:
        def body(s, carry):
            fn(s)
            return carry
        lax.fori_loop(0, nb, body, 0)


def _rows(s, L):
    start = s * L
    if not isinstance(start, int):
        start = pl.multiple_of(start, SUBLANES)
    return pl.ds(start, L)


def _rmsnorm_kernel(x_ref, w_ref, o_ref):
    x = x_ref[...]
    y = x * lax.rsqrt(jnp.mean(x * x, axis=-1, keepdims=True) + NORM_EPS)
    o_ref[...] = (y * w_ref[...]).astype(o_ref.dtype)


def _rmsnorm(x, w, out_dtype, tm=ROWS_NORM):
    n, d = x.shape
    return pl.pallas_call(
        _rmsnorm_kernel,
        grid=(n // tm,),
        in_specs=[pl.BlockSpec((tm, d), lambda i: (i, 0)), pl.BlockSpec((1, d), lambda i: (0, 0))],
        out_specs=pl.BlockSpec((tm, d), lambda i: (i, 0)),
        out_shape=jax.ShapeDtypeStruct((n, d), out_dtype),
        compiler_params=_params(("parallel",)),
        name="rmsnorm",
    )(x, w.reshape(1, d))


def _final_norm_kernel(n_first, x_ref, w_ref, a_ref, b_ref):
    x = x_ref[...]
    y = x * lax.rsqrt(jnp.mean(x * x, axis=-1, keepdims=True) + NORM_EPS) * w_ref[...]
    i = pl.program_id(0)

    @pl.when(i < n_first)
    def _():
        a_ref[...] = y

    @pl.when(i >= n_first)
    def _():
        b_ref[...] = y


def _final_norm(x, w, rows_a, rows_b, tm=ROWS_FINAL_NORM):
    d = x.shape[1]
    na, nb = rows_a // tm, rows_b // tm
    return pl.pallas_call(
        functools.partial(_final_norm_kernel, na),
        grid=(na + nb,),
        in_specs=[pl.BlockSpec((tm, d), lambda i: (i, 0)), pl.BlockSpec((1, d), lambda i: (0, 0))],
        out_specs=[pl.BlockSpec((tm, d), lambda i: (jnp.minimum(i, na - 1), 0)),
                   pl.BlockSpec((tm, d), lambda i: (jnp.maximum(i - na, 0), 0))],
        out_shape=[jax.ShapeDtypeStruct((rows_a, d), F32), jax.ShapeDtypeStruct((rows_b, d), F32)],
        compiler_params=_params(("arbitrary",)),
        name="final_norm",
    )(x, w.reshape(1, d))


def _layer_cols(layer, k, tn, col_block):
    return pl.BlockSpec((None, k, tn), lambda i, j: (layer, 0, col_block(j)))


def _in_proj_a_kernel(a_ref, w_ref, o_ref):
    o_ref[...] = jnp.dot(a_ref[...], w_ref[...].astype(BF16), preferred_element_type=F32)


def _in_proj_s_kernel(a_ref, w_ref, wnext_ref, o_ref, w_s):
    k, tn = w_ref.shape
    for k0 in range(0, k, IN_PROJ_K_CHUNK):
        rows = slice(k0, k0 + IN_PROJ_K_CHUNK)
        wide = jnp.concatenate([w_ref[rows, :], wnext_ref[rows, :]], axis=1)
        w_s[rows, :] = pltpu.roll(wide, tn + LANES - SRC_SHIFT, 1)[:, :tn].astype(BF16)
    o_ref[...] = jnp.dot(a_ref[...], w_s[...], preferred_element_type=F32)


def _in_proj_gates_kernel(a_ref, wm_ref, wg_ref, o_ref):
    lane = lax.broadcasted_iota(jnp.int32, wm_ref.shape, 1)
    w = jnp.where(lane < GATE_GB, wm_ref[...], jnp.where(lane < N_GATES, wg_ref[...], 0.0))
    o_ref[...] = jnp.dot(a_ref[...], w.astype(BF16), preferred_element_type=F32)


def _in_proj(hn, w_in, layer, tm=ROWS_DENSE, tn=COLS_DENSE):
    m, k = hn.shape
    a_spec = pl.BlockSpec((tm, k), lambda i, j: (i, 0))
    out_spec = pl.BlockSpec((tm, tn), lambda i, j: (i, j))
    proj_a = pl.pallas_call(
        _in_proj_a_kernel,
        grid=(m // tm, N_A // tn),
        in_specs=[a_spec, _layer_cols(layer, k, tn, lambda j: j)],
        out_specs=out_spec,
        out_shape=jax.ShapeDtypeStruct((m, N_A), F32),
        compiler_params=_params(("parallel", "arbitrary")),
        name="in_proj_a",
    )(hn, w_in)
    first = (SRC_S - SRC_SHIFT) // tn
    proj_s = pl.pallas_call(
        _in_proj_s_kernel,
        grid=(m // tm, N_S // tn),
        in_specs=[a_spec, _layer_cols(layer, k, tn, lambda j: first + j),
                  _layer_cols(layer, k, LANES, lambda j: (first + j + 1) * (tn // LANES))],
        out_specs=out_spec,
        out_shape=jax.ShapeDtypeStruct((m, N_S), F32),
        scratch_shapes=[pltpu.VMEM((k, tn), BF16)],
        compiler_params=_params(("parallel", "arbitrary")),
        name="in_proj_s",
    )(hn, w_in, w_in)
    gates = pl.pallas_call(
        _in_proj_gates_kernel,
        grid=(m // tm, 1),
        in_specs=[a_spec, _layer_cols(layer, k, LANES, lambda j: SRC_GATES_M // LANES),
                  _layer_cols(layer, k, LANES, lambda j: SRC_GATES_G // LANES)],
        out_specs=pl.BlockSpec((tm, LANES), lambda i, j: (i, 0)),
        out_shape=jax.ShapeDtypeStruct((m, LANES), F32),
        compiler_params=_params(("parallel", "arbitrary")),
        name="in_proj_gates",
    )(hn, w_in, w_in)
    return proj_a, proj_s, gates


def _mm_swiglu_kernel(a_ref, wg_ref, wu_ref, o_ref):
    a = a_ref[...]
    g = jnp.dot(a, wg_ref[...].astype(BF16), preferred_element_type=F32)
    u = jnp.dot(a, wu_ref[...].astype(BF16), preferred_element_type=F32)
    o_ref[...] = (_silu(g) * u).astype(o_ref.dtype)


def _matmul_swiglu(a, wg, wu, layer, tm=ROWS_DENSE, tn=COLS_DENSE):
    m, k = a.shape
    n = wg.shape[2]
    return pl.pallas_call(
        _mm_swiglu_kernel,
        grid=(m // tm, n // tn),
        in_specs=[pl.BlockSpec((tm, k), lambda i, j: (i, 0)),
                  _layer_cols(layer, k, tn, lambda j: j), _layer_cols(layer, k, tn, lambda j: j)],
        out_specs=pl.BlockSpec((tm, tn), lambda i, j: (i, j)),
        out_shape=jax.ShapeDtypeStruct((m, n), BF16),
        compiler_params=_params(("parallel", "arbitrary")),
        name="ffn_gate_up",
    )(a, wg, wu)


def _mm_resid_kernel(widths, *refs):
    a_refs = refs[:len(widths)]
    w_ref, x_ref, o_ref = refs[len(widths):]
    acc = x_ref[...]
    k0 = 0
    for a_ref, kw in zip(a_refs, widths):
        acc = acc + jnp.dot(a_ref[...], w_ref[k0:k0 + kw, :].astype(BF16), preferred_element_type=F32)
        k0 += kw
    o_ref[...] = acc


def _matmul_resid(a_list, w, layer, x, tm, tn, name):
    m = x.shape[0]
    _, k, n = w.shape
    widths = tuple(a.shape[1] for a in a_list)
    return pl.pallas_call(
        functools.partial(_mm_resid_kernel, widths),
        grid=(m // tm, n // tn),
        in_specs=[pl.BlockSpec((tm, kw), lambda i, j: (i, 0)) for kw in widths]
        + [_layer_cols(layer, k, tn, lambda j: j), pl.BlockSpec((tm, tn), lambda i, j: (i, j))],
        out_specs=pl.BlockSpec((tm, tn), lambda i, j: (i, j)),
        out_shape=jax.ShapeDtypeStruct((m, n), F32),
        compiler_params=_params(("parallel", "arbitrary")),
        name=name,
    )(*a_list, w, x)


def _rglru_kernel(nb, L, _y_all_ref, rx_ref, rg_ref, hist0_ref, h0_ref, cw_ref, cb_ref, wa_ref, ba_ref, wx_ref,
                  bx_ref, lam_ref, y_ref, h_ref, hist_ref, a_s, u_s):
    @pl.when(pl.program_id(1) == 0)
    def _():
        h_ref[...] = h0_ref[...]
        hist_ref[...] = hist0_ref[...]

    x = rx_ref[...]
    xc = _conv4(x, hist_ref[...], cw_ref[...], nb, L) + cb_ref[...]
    hist_ref[...] = _next_hist(x, nb, L)

    xcb = xc.astype(BF16)
    ga = jnp.concatenate([jnp.dot(xcb[:, n * R_BDIM:(n + 1) * R_BDIM], wa_ref[n], preferred_element_type=F32)
                          for n in range(R_BLOCKS)], axis=1)
    gx = jnp.concatenate([jnp.dot(xcb[:, n * R_BDIM:(n + 1) * R_BDIM], wx_ref[n], preferred_element_type=F32)
                          for n in range(R_BLOCKS)], axis=1)
    r = jax.nn.sigmoid(ga + ba_ref[...])
    i = jax.nn.sigmoid(gx + bx_ref[...])
    log_a = -R_C * r * _softplus(-lam_ref[...])
    a_s[...] = jnp.exp(log_a)
    th = jnp.tanh(log_a)
    u_s[...] = jnp.sqrt(-2.0 * th / (1.0 - th)) * (i * xc)

    if nb == 1:
        def step(t, h):
            h = a_s[pl.ds(t, 1), :] * h + u_s[pl.ds(t, 1), :]
            u_s[pl.ds(t, 1), :] = h
            return h

        h_last = lax.fori_loop(0, L, step, h_ref[0:1, :], unroll=SUBLANES)
        h_ref[...] = jnp.broadcast_to(h_last, (SUBLANES, R_WIDTH))
        hs = u_s[...]
    else:
        a, u = a_s[...], u_s[...]
        tpos = lax.broadcasted_iota(jnp.int32, (nb * L, 1), 0) % L
        step = 1
        while step < L:
            valid = tpos >= step
            u = jnp.where(valid, a * pltpu.roll(u, step, 0) + u, u)
            a = jnp.where(valid, a * pltpu.roll(a, step, 0), a)
            step *= 2
        if L == SUBLANES:
            h0 = h_ref[...]
        else:
            h0 = jnp.concatenate([jnp.broadcast_to(h_ref[s * SUBLANES:s * SUBLANES + 1, :], (L, R_WIDTH))
                                  for s in range(nb)], axis=0)
        hs = a * h0 + u
        if L == SUBLANES:
            h_ref[...] = pltpu.roll(hs, nb * L - (L - 1), 0)
        else:
            h_ref[...] = jnp.concatenate([jnp.broadcast_to(hs[(s + 1) * L - 1:(s + 1) * L, :], (SUBLANES, R_WIDTH))
                                          for s in range(nb)], axis=0)
    y_ref[...] = (hs * jax.nn.gelu(rg_ref[...])).astype(y_ref.dtype)


def _rglru_group(pm, y_all, row0, nseq, nb, L, nch, hist0, h0, prm):
    R = nb * L
    base = row0 // R
    nsb = nseq // nb
    rb = lambda sb, c: base + sb * nch + c
    full = lambda shape: pl.BlockSpec(shape, lambda sb, c: (0,) * len(shape))
    st = pl.BlockSpec((nb * SUBLANES, R_WIDTH), lambda sb, c: (sb, 0))
    return pl.pallas_call(
        functools.partial(_rglru_kernel, nb, L),
        grid=(nsb, nch),
        in_specs=[pl.BlockSpec(memory_space=pl.ANY),
                  pl.BlockSpec((R, R_WIDTH), lambda sb, c: (rb(sb, c), COL_RX // R_WIDTH)),
                  pl.BlockSpec((R, R_WIDTH), lambda sb, c: (rb(sb, c), COL_RG // R_WIDTH)),
                  st, st,
                  full((CONV_W, R_WIDTH)), full((1, R_WIDTH)),
                  full((R_BLOCKS, R_BDIM, R_BDIM)), full((1, R_WIDTH)),
                  full((R_BLOCKS, R_BDIM, R_BDIM)), full((1, R_WIDTH)), full((1, R_WIDTH))],
        out_specs=[pl.BlockSpec((R, R_WIDTH), lambda sb, c: (rb(sb, c), 0)), st, st],
        out_shape=[jax.ShapeDtypeStruct(y_all.shape, y_all.dtype),
                   jax.ShapeDtypeStruct((nseq * SUBLANES, R_WIDTH), F32),
                   jax.ShapeDtypeStruct((nseq * SUBLANES, R_WIDTH), F32)],
        scratch_shapes=[pltpu.VMEM((R, R_WIDTH), F32), pltpu.VMEM((R, R_WIDTH), F32)],
        compiler_params=_params(("parallel", "arbitrary")),
        input_output_aliases={0: 0},
        name="rglru",
    )(y_all, pm, pm, hist0, h0, *prm)


def _mlstm_kernel(nb, L, n_passthrough, *refs):
    (q_ref, k_ref, v_ref, o_ref, g_ref, bias_ref, nw_ref, c0_ref, n0_ref, m0_ref,
     y_ref, c_ref, n_ref, m_ref, num_s, col_s) = refs[n_passthrough:]
    R = nb * L

    @pl.when(pl.program_id(1) == 0)
    def _():
        c_ref[...] = c0_ref[...]
        n_ref[...] = n0_ref[...]
        m_ref[...] = m0_ref[...]

    z = g_ref[...] + bias_ref[...]
    lf = pltpu.roll(-_softplus(-z), LANES - GATE_MF, 1)
    causal, _ = _seq_masks(nb, L)
    fcum = _cumsum_rows(lf, nb, L)
    bcol = z - fcum
    brow = bcol.T
    if L == SUBLANES:
        m_rows = m_ref[...]
    else:
        m_rows = jnp.concatenate(
            [jnp.broadcast_to(m_ref[s * SUBLANES:s * SUBLANES + 1, :], (L, LANES)) for s in range(nb)], axis=0)

    for h in range(M_HEADS):
        fc = fcum[:, h:h + 1]
        d = fc + brow[h:h + 1, :]
        rowmax = jnp.max(jnp.where(causal, d, -jnp.inf), axis=-1, keepdims=True)
        inter = m_rows[:, h:h + 1] + fc
        mt = jnp.maximum(rowmax, inter)
        p = jnp.where(causal, jnp.exp(jnp.where(causal, d - mt, 0.0)), 0.0)
        qh = q_ref[:, h * M_DK:(h + 1) * M_DK] * (M_DK ** -0.5)
        s = _dot_nt(qh, k_ref[:, h * M_DK:(h + 1) * M_DK]) * p
        num_s[:, h * M_DV:(h + 1) * M_DV] = _dot(s, v_ref[:, h * M_DV:(h + 1) * M_DV])
        for idx, val in enumerate((mt, jnp.exp(inter - mt), jnp.sum(s, axis=-1, keepdims=True),
                                   bcol[:, h:h + 1], fc)):
            col_s[idx, h] = jnp.broadcast_to(val, (R, LANES))

    def per_seq(s):
        rows = _rows(s, L)
        mrow = pl.ds(s * SUBLANES, 1)
        for h in range(M_HEADS):
            dk = slice(h * M_DK, (h + 1) * M_DK)
            dv = slice(h * M_DV, (h + 1) * M_DV)
            mt = col_s[0, h, rows, 0:1]
            w_inter = col_s[1, h, rows, 0:1]
            den_in = col_s[2, h, rows, 0:1]
            bc = col_s[3, h, rows, 0:1]
            f_last = col_s[4, h, rows, 0:1][L - 1:L]
            qh = q_ref[rows, dk] * (M_DK ** -0.5)
            kh = k_ref[rows, dk]
            vh = v_ref[rows, dv]
            c_old = c_ref[s, h]
            n_old = n_ref[s, h:h + 1, :]
            m_old = m_ref[mrow, h:h + 1]
            num = w_inter * _dot(qh, c_old) + num_s[rows, dv]
            den = w_inter * jnp.sum(qh * n_old, axis=-1, keepdims=True) + den_in
            hh = num / jnp.maximum(jnp.abs(den), jnp.exp(-mt))
            hn = hh * lax.rsqrt(jnp.mean(hh * hh, axis=-1, keepdims=True) + NORM_EPS) * nw_ref[:, dv]
            y_ref[rows, dv] = (hn * jax.nn.sigmoid(o_ref[rows, dv])).astype(y_ref.dtype)
            m_new = mt[L - 1:L]
            kw = kh * jnp.exp(bc + (f_last - m_new))
            scale = jnp.exp(m_old + f_last - m_new)
            c_ref[s, h] = scale * c_old + _dot_tn(kw, vh)
            n_ref[s, h:h + 1, :] = scale * n_old + jnp.sum(kw, axis=0, keepdims=True)
            m_ref[pl.ds(s * SUBLANES, SUBLANES), h:h + 1] = jnp.broadcast_to(m_new, (SUBLANES, 1))

    _for_each_seq(nb, per_seq)


def _stacked_state(layer, block):
    zeros = (0,) * (len(block) - 1)
    return pl.BlockSpec((None,) + block, lambda sb, c: (layer, sb) + zeros)


def _alias_inputs(y_all, state_out):
    prev, _, _ = state_out
    operands = [y_all] + ([prev] if prev is not None else [])
    return operands, {i: i for i in range(len(operands))}


def _mlstm_group(pm, pg, y_all, row0, nseq, nb, L, nch, c_in, n0, m0, bias_row, norm_row, c_out):
    R = nb * L
    base = row0 // R
    nsb = nseq // nb
    rb = lambda sb, c: base + sb * nch + c
    full = lambda shape: pl.BlockSpec(shape, lambda sb, c: (0,) * len(shape))
    cblock = (nb, M_HEADS, M_DK, M_DV)
    nst = pl.BlockSpec((nb, M_HEADS, M_DK), lambda sb, c: (sb, 0, 0))
    mst = pl.BlockSpec((nb * SUBLANES, LANES), lambda sb, c: (sb, 0))
    passthrough, aliases = _alias_inputs(y_all, c_out)
    return pl.pallas_call(
        functools.partial(_mlstm_kernel, nb, L, len(passthrough)),
        grid=(nsb, nch),
        in_specs=[pl.BlockSpec(memory_space=pl.ANY)] * len(passthrough) + [
                  pl.BlockSpec((R, M_HEADS * M_DK), lambda sb, c: (rb(sb, c), COL_MQ // (M_HEADS * M_DK))),
                  pl.BlockSpec((R, M_HEADS * M_DK), lambda sb, c: (rb(sb, c), COL_MK // (M_HEADS * M_DK))),
                  pl.BlockSpec((R, M_WIDTH), lambda sb, c: (rb(sb, c), COL_MV // M_WIDTH)),
                  pl.BlockSpec((R, M_WIDTH), lambda sb, c: (rb(sb, c), COL_MO // M_WIDTH)),
                  pl.BlockSpec((R, LANES), lambda sb, c: (rb(sb, c), 0)),
                  full((1, LANES)), full((1, M_WIDTH)), _stacked_state(c_in[1], cblock), nst, mst],
        out_specs=[pl.BlockSpec((R, M_WIDTH), lambda sb, c: (rb(sb, c), 0)),
                   _stacked_state(c_out[1], cblock), nst, mst],
        out_shape=[jax.ShapeDtypeStruct(y_all.shape, y_all.dtype),
                   jax.ShapeDtypeStruct((c_out[2], nseq, M_HEADS, M_DK, M_DV), F32),
                   jax.ShapeDtypeStruct((nseq, M_HEADS, M_DK), F32),
                   jax.ShapeDtypeStruct((nseq * SUBLANES, LANES), F32)],
        scratch_shapes=[pltpu.VMEM((R, M_WIDTH), F32), pltpu.VMEM((5, M_HEADS, R, LANES), F32)],
        compiler_params=_params(("parallel", "arbitrary")),
        input_output_aliases=aliases,
        name="mlstm",
    )(*passthrough, pm, pm, pm, pm, pg, bias_row, norm_row, c_in[0], n0, m0)


GDN_HEAD_GROUP = 16


def _gdn_kernel(nb, L, n_passthrough, *refs):
    (q_ref, k_ref, v_ref, z_ref, g_ref, cw_ref, hist0_ref, alog_ref, dtb_ref, nw_ref, s0_ref,
     y_ref, s_ref, hist_ref, k_s, kq_s, vb_s, t_s, at_s, ws_s, vn_s, g_s) = refs[n_passthrough:]
    R = nb * L

    @pl.when(pl.program_id(1) == 0)
    def _():
        s_ref[...] = s0_ref[...]
        hist_ref[...] = hist0_ref[...]

    zt = g_ref[...]
    beta = pltpu.roll(jax.nn.sigmoid(zt), LANES - GATE_GB, 1)
    gdec = pltpu.roll(-jnp.exp(alog_ref[...]) * _softplus(zt + dtb_ref[...]), LANES - GATE_GA, 1)
    causal, strict = _seq_masks(nb, L)
    gcum = _cumsum_rows(gdec, nb, L)
    g_s[...] = gcum
    grow = gcum.T

    eye = (lax.broadcasted_iota(jnp.int32, (R, R), 0) == lax.broadcasted_iota(jnp.int32, (R, R), 1)).astype(F32)
    levels = int(math.log2(L)) - 1
    for h0 in range(0, G_HEADS, GDN_HEAD_GROUP):
        heads = range(h0, h0 + GDN_HEAD_GROUP)
        lhs, ks, decays = [], [], []
        for h in heads:
            sl = slice(h * G_DK, (h + 1) * G_DK)
            parts = []
            for idx, ref in enumerate((q_ref, k_ref, v_ref)):
                cs = slice(idx * G_WIDTH + h * G_DK, idx * G_WIDTH + (h + 1) * G_DK)
                parts.append(_silu(_conv4(ref[:, sl], hist_ref[:, cs], cw_ref[:, cs], nb, L)))
            qh, kh, vh = parts
            q = qh * lax.rsqrt(jnp.sum(qh * qh, axis=-1, keepdims=True) + NORM_EPS) * (G_DK ** -0.5)
            k = kh * lax.rsqrt(jnp.sum(kh * kh, axis=-1, keepdims=True) + NORM_EPS)
            bc = beta[:, h:h + 1]
            gc = gcum[:, h:h + 1]
            decays.append(jnp.where(causal, jnp.exp(jnp.where(causal, gc - grow[h:h + 1, :], 0.0)), 0.0))
            kb = k * bc
            eg = jnp.exp(gc)
            k_s[h] = k
            kq_s[h, 0] = kb * eg
            kq_s[h, 1] = q * eg
            vb_s[h] = vh * bc
            lhs.append(jnp.concatenate([kb, q], axis=0))
            ks.append(k)
        aas = [_dot_nt(l, k) for l, k in zip(lhs, ks)]
        avs = [jnp.where(strict, aa[:R] * d, 0.0) for aa, d in zip(aas, decays)]
        for h, aa, d in zip(heads, aas, decays):
            at_s[h] = aa[R:] * d
        ps = [eye - a for a in avs]
        bs = [_dot(a, a) for a in avs]
        for lvl in range(levels):
            if lvl + 1 < levels:
                xs = [_dot(b, jnp.concatenate([b, p], axis=1)) for b, p in zip(bs, ps)]
                bs = [x[:, :R] for x in xs]
                ps = [p + x[:, R:] for p, x in zip(ps, xs)]
            else:
                ps = [p + _dot(b, p) for b, p in zip(bs, ps)]
        for h, p in zip(heads, ps):
            t_s[h] = p

    def read_state(s):
        rows = _rows(s, L)
        for h in range(G_HEADS):
            x = _dot(jnp.concatenate([kq_s[h, 0, rows, :], kq_s[h, 1, rows, :]], axis=0), s_ref[s, h])
            ws_s[h, 0, rows, :] = x[:L]
            ws_s[h, 1, rows, :] = x[L:]

    _for_each_seq(nb, read_state)

    for h in range(G_HEADS):
        sl = slice(h * G_DK, (h + 1) * G_DK)
        v_new = _dot(t_s[h], vb_s[h] - ws_s[h, 0])
        vn_s[h] = v_new
        o = ws_s[h, 1] + _dot(at_s[h], v_new)
        on = o * lax.rsqrt(jnp.mean(o * o, axis=-1, keepdims=True) + NORM_EPS) * nw_ref[...]
        y_ref[:, sl] = (on * _silu(z_ref[:, sl])).astype(y_ref.dtype)

    def write_state(s):
        rows = _rows(s, L)
        for h in range(G_HEADS):
            gseq = g_s[rows, h:h + 1]
            g_end = gseq[L - 1:L]
            kdec = k_s[h, rows, :] * jnp.exp(g_end - gseq)
            s_ref[s, h] = s_ref[s, h] * jnp.exp(g_end) + _dot_tn(kdec, vn_s[h, rows, :])

    _for_each_seq(nb, write_state)

    for idx, ref in enumerate((q_ref, k_ref, v_ref)):
        hist_ref[:, idx * G_WIDTH:(idx + 1) * G_WIDTH] = _next_hist(ref[...], nb, L)


def _gdn_group(pm, pg, y_all, row0, nseq, nb, L, nch, s_in, hist0, prm, s_out):
    R = nb * L
    base = row0 // R
    nsb = nseq // nb
    rb = lambda sb, c: base + sb * nch + c
    full = lambda shape: pl.BlockSpec(shape, lambda sb, c: (0,) * len(shape))
    sblock = (nb, G_HEADS, G_DK, G_DV)
    hst = pl.BlockSpec((nb * SUBLANES, G_QKV), lambda sb, c: (sb, 0))
    col = lambda off: pl.BlockSpec((R, G_WIDTH), lambda sb, c: (rb(sb, c), off // G_WIDTH))
    passthrough, aliases = _alias_inputs(y_all, s_out)
    return pl.pallas_call(
        functools.partial(_gdn_kernel, nb, L, len(passthrough)),
        grid=(nsb, nch),
        in_specs=[pl.BlockSpec(memory_space=pl.ANY)] * len(passthrough) + [
                  col(COL_GQ), col(COL_GK), col(COL_GV), col(COL_GZ),
                  pl.BlockSpec((R, LANES), lambda sb, c: (rb(sb, c), 0)),
                  full((CONV_W, G_QKV)), hst, full((1, LANES)), full((1, LANES)), full((1, G_DV)),
                  _stacked_state(s_in[1], sblock)],
        out_specs=[pl.BlockSpec((R, G_WIDTH), lambda sb, c: (rb(sb, c), 0)),
                   _stacked_state(s_out[1], sblock), hst],
        out_shape=[jax.ShapeDtypeStruct(y_all.shape, y_all.dtype),
                   jax.ShapeDtypeStruct((s_out[2], nseq, G_HEADS, G_DK, G_DV), F32),
                   jax.ShapeDtypeStruct((nseq * SUBLANES, G_QKV), F32)],
        scratch_shapes=[pltpu.VMEM((G_HEADS, R, G_DK), F32), pltpu.VMEM((G_HEADS, 2, R, G_DK), F32),
                        pltpu.VMEM((G_HEADS, R, G_DV), F32), pltpu.VMEM((G_HEADS, R, R), F32),
                        pltpu.VMEM((G_HEADS, R, R), F32), pltpu.VMEM((G_HEADS, 2, R, G_DV), F32),
                        pltpu.VMEM((G_HEADS, R, G_DV), F32), pltpu.VMEM((R, LANES), F32)],
        compiler_params=_params(("parallel", "arbitrary")),
        input_output_aliases=aliases,
        name="gdn",
    )(*passthrough, pm, pm, pm, pm, pg, prm[0], hist0, prm[1], prm[2], prm[3], s_in[0])


def _lane_row(vec, offset):
    return jnp.zeros((1, LANES), F32).at[0, offset:offset + vec.shape[0]].set(vec.astype(F32))


def _pad_hist(buf):
    n, w, c = buf.shape
    return jnp.pad(buf.astype(F32), ((0, 0), (0, SUBLANES - w), (0, 0))).reshape(n * SUBLANES, c)


def _unpad_hist(hist, n):
    return hist.reshape(n, SUBLANES, hist.shape[-1])[:, :CONV_W - 1]


def _mixers(proj, mix, group, states, big_out, lw):
    pa, ps, pg = proj
    row0, nseq, nb_m, nb_r, nb_g, L_m, L_r, L_g, T = group
    c_in, n0, m0, h0, rhist, s_in, ghist = states
    ym, c, n, m = _mlstm_group(pa, pg, mix[0], row0, nseq, nb_m, L_m, T // L_m, c_in, n0, m0,
                               lw["m_bias"], lw["m_norm"], big_out[0])
    yr, h, rh = _rglru_group(ps, mix[1], row0, nseq, nb_r, L_r, T // L_r, rhist, h0, lw["r_prm"])
    yg, s, gh = _gdn_group(ps, pg, mix[2], row0, nseq, nb_g, L_g, T // L_g, s_in, ghist, lw["g_prm"], big_out[1])
    return (ym, yr, yg), (c, n, m, h, rh, s, gh)


def kernel(x_prompt, x_sample, state_mlstm_C, state_mlstm_n, state_mlstm_m, state_rglru_h, state_rglru_conv, state_gdn_S, state_gdn_conv, meta_tokens, norm_mix, w_in, m_bias_i, m_bias_f, m_norm, r_conv_w, r_conv_b, r_gate_a_w, r_gate_a_b, r_gate_x_w, r_gate_x_b, r_lambda, g_conv_w, g_A_log, g_dt_bias, g_norm, w_out, norm_ffn, w_gate, w_up, w_down, norm_final):
    batch, seq, d = x_prompt.shape
    dec_batch, dec_seq, _ = x_sample.shape
    depth = w_in.shape[0]
    n_prompt = batch * seq
    n_sample = dec_batch * dec_seq
    n_rows = n_prompt + n_sample + batch * N_META

    meta = jnp.broadcast_to(meta_tokens.astype(F32)[None], (batch, N_META, d))
    x = jnp.concatenate([x_prompt.reshape(n_prompt, d), x_sample.reshape(n_sample, d),
                         meta.reshape(batch * N_META, d)], axis=0)

    g_meta = (n_prompt + n_sample, batch, batch, batch, batch, N_META, N_META, N_META, N_META)
    g_prompt = (0, batch, 1, 1, 1, 256, 256, 64, seq)
    g_sample = (n_prompt, dec_batch, 16, 32, 8, dec_seq, dec_seq, dec_seq, dec_seq)

    zeros = lambda *shape: jnp.zeros(shape, F32)
    meta_states = ((zeros(1, batch, M_HEADS, M_DK, M_DV), 0), zeros(batch, M_HEADS, M_DK),
                   zeros(batch * SUBLANES, LANES), zeros(batch * SUBLANES, R_WIDTH), zeros(batch * SUBLANES, R_WIDTH),
                   (zeros(1, batch, G_HEADS, G_DK, G_DV), 0), zeros(batch * SUBLANES, G_QKV))
    one_layer = ((None, 0, 1), (None, 0, 1))
    w_down_bf16 = w_down.astype(BF16)

    prompt_new, sample_new = [], []
    sample_c = sample_s = None
    for l in range(depth):
        lw = dict(
            m_bias=_lane_row(m_bias_i[l], GATE_MI) + _lane_row(m_bias_f[l], GATE_MF),
            m_norm=m_norm[l].reshape(1, M_WIDTH),
            r_prm=(r_conv_w[l], r_conv_b[l].reshape(1, R_WIDTH), r_gate_a_w[l].astype(BF16),
                   r_gate_a_b[l].reshape(1, R_WIDTH), r_gate_x_w[l].astype(BF16),
                   r_gate_x_b[l].reshape(1, R_WIDTH), r_lambda[l].reshape(1, R_WIDTH)),
            g_prm=(g_conv_w[l], _lane_row(g_A_log[l], GATE_GA), _lane_row(g_dt_bias[l], GATE_GA),
                   g_norm[l].reshape(1, G_DV)),
        )
        sample_states = ((state_mlstm_C, l), state_mlstm_n[l],
                         jnp.pad(jnp.repeat(state_mlstm_m[l], SUBLANES, axis=0), ((0, 0), (0, LANES - M_HEADS))),
                         jnp.repeat(state_rglru_h[l], SUBLANES, axis=0), _pad_hist(state_rglru_conv[l]),
                         (state_gdn_S, l), _pad_hist(state_gdn_conv[l]))

        hn = _rmsnorm(x, norm_mix[l], BF16)
        proj = _in_proj(hn, w_in, l)

        def as_input(st):
            return ((st[0], 0),) + st[1:5] + ((st[5], 0), st[6])

        mix = tuple(jnp.zeros((n_rows, width), BF16) for width in (M_WIDTH, R_WIDTH, G_WIDTH))
        mix, st_meta = _mixers(proj, mix, g_meta, meta_states, one_layer, lw)
        mix, st_prompt = _mixers(proj, mix, g_prompt, as_input(st_meta), one_layer, lw)
        mix, st_sample = _mixers(proj, mix, g_sample, sample_states,
                                 ((sample_c, l, depth), (sample_s, l, depth)), lw)
        sample_c, sample_s = st_sample[0], st_sample[5]
        prompt_new.append(st_prompt)
        sample_new.append(st_sample)

        x = _matmul_resid(mix, w_out, l, x, ROWS_DENSE, COLS_DENSE, "out_proj")
        hf = _rmsnorm(x, norm_ffn[l], BF16)
        hmid = _matmul_swiglu(hf, w_gate, w_up, l)
        x = _matmul_resid([hmid], w_down_bf16, l, x, ROWS_DOWN, COLS_DOWN, "ffn_down")

    y_prompt_out, y_sample_out = _final_norm(x, norm_final, n_prompt, n_sample)
    y_prompt_out = y_prompt_out.reshape(batch, seq, d)
    y_sample_out = y_sample_out.reshape(dec_batch, dec_seq, d)

    def small_states(per_layer, n):
        nn, m, h, rh, gh = (jnp.stack([st[i] for st in per_layer]) for i in (1, 2, 3, 4, 6))
        return (nn, m[:, ::SUBLANES, :M_HEADS], h[:, ::SUBLANES],
                jax.vmap(lambda a: _unpad_hist(a, n))(rh), jax.vmap(lambda a: _unpad_hist(a, n))(gh))

    pn, pm_, ph, prh, pgh = small_states(prompt_new, batch)
    sn, sm, sh, srh, sgh = small_states(sample_new, dec_batch)
    prompt_c = jnp.concatenate([st[0] for st in prompt_new], axis=0)
    prompt_s = jnp.concatenate([st[5] for st in prompt_new], axis=0)
    return (y_prompt_out, y_sample_out, prompt_c, pn, pm_, ph, prh, prompt_s, pgh,
            sample_c, sn, sm, sh, srh, sample_s, sgh)
```

```python
import functools
import math

import jax
import jax.numpy as jnp
from jax import lax
from jax.experimental import pallas as pl
from jax.experimental.pallas import tpu as pltpu

F32 = jnp.float32
BF16 = jnp.bfloat16

N_META = 16
CONV_W = 4
NORM_EPS = 1e-6
M_HEADS = 4
M_DV = 256
M_DK = 128
M_WIDTH = M_HEADS * M_DV
R_WIDTH = 1024
R_BLOCKS = 8
R_BDIM = R_WIDTH // R_BLOCKS
R_C = 8.0
G_DK = 128
G_DV = 128
G_HEADS = 16
G_WIDTH = G_HEADS * G_DV
G_QKV = 3 * G_WIDTH

LANES = 128
SUBLANES = 8
VMEM_LIMIT = 56 * 2 ** 20

SRC_GATES_M = 3072
SRC_S = 3080
SRC_SHIFT = SRC_S % LANES
N_A = 3072
N_S = 10240
SRC_GATES_G = SRC_S + N_S - SRC_SHIFT
N_IN = SRC_S + N_S + 32
COL_MQ, COL_MK, COL_MV, COL_MO = 0, 512, 1024, 2048
COL_RX, COL_RG, COL_GQ, COL_GK, COL_GV, COL_GZ = 0, 1024, 2048, 4096, 6144, 8192
GATE_MI, GATE_MF, GATE_GB, GATE_GA = 0, 4, 8, 24
N_GATES = 40

ROWS_DENSE = 1856
ROWS_DOWN = 464
COLS_DENSE = 256
COLS_DOWN = 512
ROWS_NORM = 464
ROWS_FINAL_NORM = 512


def _params(semantics):
    return pltpu.CompilerParams(dimension_semantics=semantics, vmem_limit_bytes=VMEM_LIMIT)


def _dot(a, b):
    return jnp.dot(a.astype(BF16), b.astype(BF16), preferred_element_type=F32)


def _dot_nt(a, b):
    return lax.dot_general(a.astype(BF16), b.astype(BF16), (((1,), (1,)), ((), ())),
                           preferred_element_type=F32)


def _dot_tn(a, b):
    return lax.dot_general(a.astype(BF16), b.astype(BF16), (((0,), (0,)), ((), ())),
                           preferred_element_type=F32)


def _softplus(x):
    return jnp.maximum(x, 0.0) + jnp.log1p(jnp.exp(-jnp.abs(x)))


def _silu(x):
    return x * jax.nn.sigmoid(x)


def _seq_masks(nb, L):
    R = nb * L
    row = lax.broadcasted_iota(jnp.int32, (R, R), 0)
    col = lax.broadcasted_iota(jnp.int32, (R, R), 1)
    causal = row >= col
    strict = row > col
    if nb > 1:
        same = (row // L) == (col // L)
        causal = jnp.logical_and(causal, same)
        strict = jnp.logical_and(strict, same)
    return causal, strict


def _cumsum_rows(x, nb, L):
    tpos = lax.broadcasted_iota(jnp.int32, (nb * L, 1), 0) % L
    step = 1
    while step < L:
        x = x + jnp.where(tpos >= step, pltpu.roll(x, step, 0), 0.0)
        step *= 2
    return x


def _conv4(x, hist, w, nb, L):
    R = nb * L
    acc = x * w[CONV_W - 1:CONV_W, :]
    if L == SUBLANES:
        tpos = lax.broadcasted_iota(jnp.int32, (R, 1), 0) % L
    else:
        tpos = lax.broadcasted_iota(jnp.int32, (SUBLANES, 1), 0)
    for j in range(1, CONV_W):
        xs = pltpu.roll(x, j, 0)
        shift = (j - (CONV_W - 1)) % (nb * SUBLANES)
        hr = hist if shift == 0 else pltpu.roll(hist, shift, 0)
        if L == SUBLANES:
            sh = jnp.where(tpos < j, hr, xs)
        else:
            parts = []
            for s in range(nb):
                head = jnp.where(tpos < j, hr[s * SUBLANES:(s + 1) * SUBLANES],
                                 xs[s * L:s * L + SUBLANES])
                parts += [head, xs[s * L + SUBLANES:(s + 1) * L]]
            sh = jnp.concatenate(parts, axis=0)
        acc = acc + sh * w[CONV_W - 1 - j:CONV_W - j, :]
    return acc


def _next_hist(x, nb, L):
    if L == SUBLANES:
        return pltpu.roll(x, nb * L - (SUBLANES - (CONV_W - 1)), 0)
    parts = [pltpu.roll(x[(s + 1) * L - SUBLANES:(s + 1) * L], CONV_W - 1, 0) for s in range(nb)]
    return parts[0] if nb == 1 else jnp.concatenate(parts, axis=0)


def _for_each_seq(nb, fn):
    if nb <= 4:
        for s in range(nb):
            fn(s)
    else:
        def body(s, carry):
            fn(s)
            return carry
        lax.fori_loop(0, nb, body, 0)


def _rows(s, L):
    start = s * L
    if not isinstance(start, int):
        start = pl.multiple_of(start, SUBLANES)
    return pl.ds(start, L)


def _rmsnorm_kernel(x_ref, w_ref, o_ref):
    x = x_ref[...]
    y = x * lax.rsqrt(jnp.mean(x * x, axis=-1, keepdims=True) + NORM_EPS)
    o_ref[...] = (y * w_ref[...]).astype(o_ref.dtype)


def _rmsnorm(x, w, out_dtype, tm=ROWS_NORM):
    n, d = x.shape
    return pl.pallas_call(
        _rmsnorm_kernel,
        grid=(n // tm,),
        in_specs=[pl.BlockSpec((tm, d), lambda i: (i, 0)), pl.BlockSpec((1, d), lambda i: (0, 0))],
        out_specs=pl.BlockSpec((tm, d), lambda i: (i, 0)),
        out_shape=jax.ShapeDtypeStruct((n, d), out_dtype),
        compiler_params=_params(("parallel",)),
        name="rmsnorm",
    )(x, w.reshape(1, d))


def _final_norm_kernel(n_first, x_ref, w_ref, a_ref, b_ref):
    x = x_ref[...]
    y = x * lax.rsqrt(jnp.mean(x * x, axis=-1, keepdims=True) + NORM_EPS) * w_ref[...]
    i = pl.program_id(0)

    @pl.when(i < n_first)
    def _():
        a_ref[...] = y

    @pl.when(i >= n_first)
    def _():
        b_ref[...] = y


def _final_norm(x, w, rows_a, rows_b, tm=ROWS_FINAL_NORM):
    d = x.shape[1]
    na, nb = rows_a // tm, rows_b // tm
    return pl.pallas_call(
        functools.partial(_final_norm_kernel, na),
        grid=(na + nb,),
        in_specs=[pl.BlockSpec((tm, d), lambda i: (i, 0)), pl.BlockSpec((1, d), lambda i: (0, 0))],
        out_specs=[pl.BlockSpec((tm, d), lambda i: (jnp.minimum(i, na - 1), 0)),
                   pl.BlockSpec((tm, d), lambda i: (jnp.maximum(i - na, 0), 0))],
        out_shape=[jax.ShapeDtypeStruct((rows_a, d), F32), jax.ShapeDtypeStruct((rows_b, d), F32)],
        compiler_params=_params(("arbitrary",)),
        name="final_norm",
    )(x, w.reshape(1, d))


def _layer_cols(layer, k, tn, col_block):
    return pl.BlockSpec((None, k, tn), lambda i, j: (layer, 0, col_block(j)))


def _layer_rows(layer, rows, k, row_block):
    return pl.BlockSpec((None, rows, k), lambda i, j: (layer, row_block(j), 0))


def _dot_wt(a, wt):
    return lax.dot_general(a, wt.astype(BF16), (((1,), (1,)), ((), ())), preferred_element_type=F32)


def _in_proj_a_kernel(a_ref, wt_ref, o_ref):
    o_ref[...] = _dot_wt(a_ref[...], wt_ref[...])


def _in_proj_s_kernel(a_ref, wt_ref, wnext_ref, o_ref):
    o_ref[...] = _dot_wt(a_ref[...], jnp.concatenate([wt_ref[SRC_SHIFT:, :], wnext_ref[...]], axis=0))


def _in_proj_gates_kernel(a_ref, wm_ref, g0_ref, g1_ref, g2_ref, g3_ref, o_ref):
    k = wm_ref.shape[1]
    wt = jnp.concatenate([wm_ref[...], g0_ref[...], g1_ref[...], g2_ref[...], g3_ref[...],
                          jnp.zeros((LANES - N_GATES, k), F32)], axis=0)
    o_ref[...] = _dot_wt(a_ref[...], wt)


def _in_proj(hn, w_in_t, layer, tm=ROWS_DENSE, tn=COLS_DENSE):
    m, k = hn.shape
    assert SRC_SHIFT == SUBLANES and w_in_t.shape[1:] == (N_IN, k)
    a_spec = pl.BlockSpec((tm, k), lambda i, j: (i, 0))
    out_spec = pl.BlockSpec((tm, tn), lambda i, j: (i, j))
    proj_a = pl.pallas_call(
        _in_proj_a_kernel,
        grid=(m // tm, N_A // tn),
        in_specs=[a_spec, _layer_rows(layer, tn, k, lambda j: j)],
        out_specs=out_spec,
        out_shape=jax.ShapeDtypeStruct((m, N_A), F32),
        compiler_params=_params(("parallel", "arbitrary")),
        name="in_proj_a",
    )(hn, w_in_t)
    first = (SRC_S - SRC_SHIFT) // tn
    proj_s = pl.pallas_call(
        _in_proj_s_kernel,
        grid=(m // tm, N_S // tn),
        in_specs=[a_spec, _layer_rows(layer, tn, k, lambda j: first + j),
                  _layer_rows(layer, SUBLANES, k, lambda j: (first + j + 1) * (tn // SUBLANES))],
        out_specs=out_spec,
        out_shape=jax.ShapeDtypeStruct((m, N_S), F32),
        compiler_params=_params(("parallel", "arbitrary")),
        name="in_proj_s",
    )(hn, w_in_t, w_in_t)
    gate_rows = [SRC_GATES_M] + [SRC_GATES_G + SRC_SHIFT + r * SUBLANES for r in range(4)]
    gates = pl.pallas_call(
        _in_proj_gates_kernel,
        grid=(m // tm, 1),
        in_specs=[a_spec] + [_layer_rows(layer, SUBLANES, k, lambda j, r=r: r // SUBLANES) for r in gate_rows],
        out_specs=pl.BlockSpec((tm, LANES), lambda i, j: (i, 0)),
        out_shape=jax.ShapeDtypeStruct((m, LANES), F32),
        compiler_params=_params(("parallel", "arbitrary")),
        name="in_proj_gates",
    )(hn, *([w_in_t] * len(gate_rows)))
    return proj_a, proj_s, gates


def _mm_swiglu_kernel(a_ref, wg_ref, wu_ref, o_ref):
    a = a_ref[...]
    g = jnp.dot(a, wg_ref[...].astype(BF16), preferred_element_type=F32)
    u = jnp.dot(a, wu_ref[...].astype(BF16), preferred_element_type=F32)
    o_ref[...] = (_silu(g) * u).astype(o_ref.dtype)


def _matmul_swiglu(a, wg, wu, layer, tm=ROWS_DENSE, tn=COLS_DENSE):
    m, k = a.shape
    n = wg.shape[2]
    return pl.pallas_call(
        _mm_swiglu_kernel,
        grid=(m // tm, n // tn),
        in_specs=[pl.BlockSpec((tm, k), lambda i, j: (i, 0)),
                  _layer_cols(layer, k, tn, lambda j: j), _layer_cols(layer, k, tn, lambda j: j)],
        out_specs=pl.BlockSpec((tm, tn), lambda i, j: (i, j)),
        out_shape=jax.ShapeDtypeStruct((m, n), BF16),
        compiler_params=_params(("parallel", "arbitrary")),
        name="ffn_gate_up",
    )(a, wg, wu)


def _mm_resid_kernel(widths, *refs):
    a_refs = refs[:len(widths)]
    w_ref, x_ref, o_ref = refs[len(widths):]
    acc = x_ref[...]
    k0 = 0
    for a_ref, kw in zip(a_refs, widths):
        acc = acc + jnp.dot(a_ref[...], w_ref[k0:k0 + kw, :].astype(BF16), preferred_element_type=F32)
        k0 += kw
    o_ref[...] = acc


def _matmul_resid(a_list, w, layer, x, tm, tn, name):
    m = x.shape[0]
    _, k, n = w.shape
    widths = tuple(a.shape[1] for a in a_list)
    return pl.pallas_call(
        functools.partial(_mm_resid_kernel, widths),
        grid=(m // tm, n // tn),
        in_specs=[pl.BlockSpec((tm, kw), lambda i, j: (i, 0)) for kw in widths]
        + [_layer_cols(layer, k, tn, lambda j: j), pl.BlockSpec((tm, tn), lambda i, j: (i, j))],
        out_specs=pl.BlockSpec((tm, tn), lambda i, j: (i, j)),
        out_shape=jax.ShapeDtypeStruct((m, n), F32),
        compiler_params=_params(("parallel", "arbitrary")),
        name=name,
    )(*a_list, w, x)


def _rglru_kernel(nb, L, _y_all_ref, rx_ref, rg_ref, hist0_ref, h0_ref, cw_ref, cb_ref, wa_ref, ba_ref, wx_ref,
                  bx_ref, lam_ref, y_ref, h_ref, hist_ref, a_s, u_s):
    @pl.when(pl.program_id(1) == 0)
    def _():
        h_ref[...] = h0_ref[...]
        hist_ref[...] = hist0_ref[...]

    x = rx_ref[...]
    xc = _conv4(x, hist_ref[...], cw_ref[...], nb, L) + cb_ref[...]
    hist_ref[...] = _next_hist(x, nb, L)

    xcb = xc.astype(BF16)
    ga = jnp.concatenate([jnp.dot(xcb[:, n * R_BDIM:(n + 1) * R_BDIM], wa_ref[n], preferred_element_type=F32)
                          for n in range(R_BLOCKS)], axis=1)
    gx = jnp.concatenate([jnp.dot(xcb[:, n * R_BDIM:(n + 1) * R_BDIM], wx_ref[n], preferred_element_type=F32)
                          for n in range(R_BLOCKS)], axis=1)
    r = jax.nn.sigmoid(ga + ba_ref[...])
    i = jax.nn.sigmoid(gx + bx_ref[...])
    log_a = -R_C * r * _softplus(-lam_ref[...])
    a_s[...] = jnp.exp(log_a)
    th = jnp.tanh(log_a)
    u_s[...] = jnp.sqrt(-2.0 * th / (1.0 - th)) * (i * xc)

    if nb == 1:
        def step(t, h):
            h = a_s[pl.ds(t, 1), :] * h + u_s[pl.ds(t, 1), :]
            u_s[pl.ds(t, 1), :] = h
            return h

        h_last = lax.fori_loop(0, L, step, h_ref[0:1, :], unroll=SUBLANES)
        h_ref[...] = jnp.broadcast_to(h_last, (SUBLANES, R_WIDTH))
        hs = u_s[...]
    else:
        a, u = a_s[...], u_s[...]
        tpos = lax.broadcasted_iota(jnp.int32, (nb * L, 1), 0) % L
        step = 1
        while step < L:
            valid = tpos >= step
            u = jnp.where(valid, a * pltpu.roll(u, step, 0) + u, u)
            a = jnp.where(valid, a * pltpu.roll(a, step, 0), a)
            step *= 2
        if L == SUBLANES:
            h0 = h_ref[...]
        else:
            h0 = jnp.concatenate([jnp.broadcast_to(h_ref[s * SUBLANES:s * SUBLANES + 1, :], (L, R_WIDTH))
                                  for s in range(nb)], axis=0)
        hs = a * h0 + u
        if L == SUBLANES:
            h_ref[...] = pltpu.roll(hs, nb * L - (L - 1), 0)
        else:
            h_ref[...] = jnp.concatenate([jnp.broadcast_to(hs[(s + 1) * L - 1:(s + 1) * L, :], (SUBLANES, R_WIDTH))
                                          for s in range(nb)], axis=0)
    y_ref[...] = (hs * jax.nn.gelu(rg_ref[...])).astype(y_ref.dtype)


def _rglru_group(pm, y_all, row0, nseq, nb, L, nch, hist0, h0, prm):
    R = nb * L
    base = row0 // R
    nsb = nseq // nb
    rb = lambda sb, c: base + sb * nch + c
    full = lambda shape: pl.BlockSpec(shape, lambda sb, c: (0,) * len(shape))
    st = pl.BlockSpec((nb * SUBLANES, R_WIDTH), lambda sb, c: (sb, 0))
    return pl.pallas_call(
        functools.partial(_rglru_kernel, nb, L),
        grid=(nsb, nch),
        in_specs=[pl.BlockSpec(memory_space=pl.ANY),
                  pl.BlockSpec((R, R_WIDTH), lambda sb, c: (rb(sb, c), COL_RX // R_WIDTH)),
                  pl.BlockSpec((R, R_WIDTH), lambda sb, c: (rb(sb, c), COL_RG // R_WIDTH)),
                  st, st,
                  full((CONV_W, R_WIDTH)), full((1, R_WIDTH)),
                  full((R_BLOCKS, R_BDIM, R_BDIM)), full((1, R_WIDTH)),
                  full((R_BLOCKS, R_BDIM, R_BDIM)), full((1, R_WIDTH)), full((1, R_WIDTH))],
        out_specs=[pl.BlockSpec((R, R_WIDTH), lambda sb, c: (rb(sb, c), 0)), st, st],
        out_shape=[jax.ShapeDtypeStruct(y_all.shape, y_all.dtype),
                   jax.ShapeDtypeStruct((nseq * SUBLANES, R_WIDTH), F32),
                   jax.ShapeDtypeStruct((nseq * SUBLANES, R_WIDTH), F32)],
        scratch_shapes=[pltpu.VMEM((R, R_WIDTH), F32), pltpu.VMEM((R, R_WIDTH), F32)],
        compiler_params=_params(("parallel", "arbitrary")),
        input_output_aliases={0: 0},
        name="rglru",
    )(y_all, pm, pm, hist0, h0, *prm)


def _mlstm_kernel(nb, L, n_passthrough, *refs):
    (q_ref, k_ref, v_ref, o_ref, g_ref, bias_ref, nw_ref, c0_ref, n0_ref, m0_ref,
     y_ref, c_ref, n_ref, m_ref, num_s, col_s) = refs[n_passthrough:]
    R = nb * L

    @pl.when(pl.program_id(1) == 0)
    def _():
        c_ref[...] = c0_ref[...]
        n_ref[...] = n0_ref[...]
        m_ref[...] = m0_ref[...]

    z = g_ref[...] + bias_ref[...]
    lf = pltpu.roll(-_softplus(-z), LANES - GATE_MF, 1)
    causal, _ = _seq_masks(nb, L)
    fcum = _cumsum_rows(lf, nb, L)
    bcol = z - fcum
    brow = bcol.T
    if L == SUBLANES:
        m_rows = m_ref[...]
    else:
        m_rows = jnp.concatenate(
            [jnp.broadcast_to(m_ref[s * SUBLANES:s * SUBLANES + 1, :], (L, LANES)) for s in range(nb)], axis=0)

    for h in range(M_HEADS):
        fc = fcum[:, h:h + 1]
        d = fc + brow[h:h + 1, :]
        rowmax = jnp.max(jnp.where(causal, d, -jnp.inf), axis=-1, keepdims=True)
        inter = m_rows[:, h:h + 1] + fc
        mt = jnp.maximum(rowmax, inter)
        p = jnp.where(causal, jnp.exp(jnp.where(causal, d - mt, 0.0)), 0.0)
        qh = q_ref[:, h * M_DK:(h + 1) * M_DK] * (M_DK ** -0.5)
        s = _dot_nt(qh, k_ref[:, h * M_DK:(h + 1) * M_DK]) * p
        num_s[:, h * M_DV:(h + 1) * M_DV] = _dot(s, v_ref[:, h * M_DV:(h + 1) * M_DV])
        for idx, val in enumerate((mt, jnp.exp(inter - mt), jnp.sum(s, axis=-1, keepdims=True),
                                   bcol[:, h:h + 1], fc)):
            col_s[idx, h] = jnp.broadcast_to(val, (R, LANES))

    def per_seq(s):
        rows = _rows(s, L)
        mrow = pl.ds(s * SUBLANES, 1)
        for h in range(M_HEADS):
            dk = slice(h * M_DK, (h + 1) * M_DK)
            dv = slice(h * M_DV, (h + 1) * M_DV)
            mt = col_s[0, h, rows, 0:1]
            w_inter = col_s[1, h, rows, 0:1]
            den_in = col_s[2, h, rows, 0:1]
            bc = col_s[3, h, rows, 0:1]
            f_last = col_s[4, h, rows, 0:1][L - 1:L]
            qh = q_ref[rows, dk] * (M_DK ** -0.5)
            kh = k_ref[rows, dk]
            vh = v_ref[rows, dv]
            c_old = c_ref[s, h]
            n_old = n_ref[s, h:h + 1, :]
            m_old = m_ref[mrow, h:h + 1]
            num = w_inter * _dot(qh, c_old) + num_s[rows, dv]
            den = w_inter * jnp.sum(qh * n_old, axis=-1, keepdims=True) + den_in
            hh = num / jnp.maximum(jnp.abs(den), jnp.exp(-mt))
            hn = hh * lax.rsqrt(jnp.mean(hh * hh, axis=-1, keepdims=True) + NORM_EPS) * nw_ref[:, dv]
            y_ref[rows, dv] = (hn * jax.nn.sigmoid(o_ref[rows, dv])).astype(y_ref.dtype)
            m_new = mt[L - 1:L]
            kw = kh * jnp.exp(bc + (f_last - m_new))
            scale = jnp.exp(m_old + f_last - m_new)
            c_ref[s, h] = scale * c_old + _dot_tn(kw, vh)
            n_ref[s, h:h + 1, :] = scale * n_old + jnp.sum(kw, axis=0, keepdims=True)
            m_ref[pl.ds(s * SUBLANES, SUBLANES), h:h + 1] = jnp.broadcast_to(m_new, (SUBLANES, 1))

    _for_each_seq(nb, per_seq)


def _stacked_state(layer, block):
    zeros = (0,) * (len(block) - 1)
    return pl.BlockSpec((None,) + block, lambda sb, c: (layer, sb) + zeros)


def _alias_inputs(y_all, state_out):
    prev, _, _ = state_out
    operands = [y_all] + ([prev] if prev is not None else [])
    return operands, {i: i for i in range(len(operands))}


def _mlstm_group(pm, pg, y_all, row0, nseq, nb, L, nch, c_in, n0, m0, bias_row, norm_row, c_out):
    R = nb * L
    base = row0 // R
    nsb = nseq // nb
    rb = lambda sb, c: base + sb * nch + c
    full = lambda shape: pl.BlockSpec(shape, lambda sb, c: (0,) * len(shape))
    cblock = (nb, M_HEADS, M_DK, M_DV)
    nst = pl.BlockSpec((nb, M_HEADS, M_DK), lambda sb, c: (sb, 0, 0))
    mst = pl.BlockSpec((nb * SUBLANES, LANES), lambda sb, c: (sb, 0))
    passthrough, aliases = _alias_inputs(y_all, c_out)
    return pl.pallas_call(
        functools.partial(_mlstm_kernel, nb, L, len(passthrough)),
        grid=(nsb, nch),
        in_specs=[pl.BlockSpec(memory_space=pl.ANY)] * len(passthrough) + [
                  pl.BlockSpec((R, M_HEADS * M_DK), lambda sb, c: (rb(sb, c), COL_MQ // (M_HEADS * M_DK))),
                  pl.BlockSpec((R, M_HEADS * M_DK), lambda sb, c: (rb(sb, c), COL_MK // (M_HEADS * M_DK))),
                  pl.BlockSpec((R, M_WIDTH), lambda sb, c: (rb(sb, c), COL_MV // M_WIDTH)),
                  pl.BlockSpec((R, M_WIDTH), lambda sb, c: (rb(sb, c), COL_MO // M_WIDTH)),
                  pl.BlockSpec((R, LANES), lambda sb, c: (rb(sb, c), 0)),
                  full((1, LANES)), full((1, M_WIDTH)), _stacked_state(c_in[1], cblock), nst, mst],
        out_specs=[pl.BlockSpec((R, M_WIDTH), lambda sb, c: (rb(sb, c), 0)),
                   _stacked_state(c_out[1], cblock), nst, mst],
        out_shape=[jax.ShapeDtypeStruct(y_all.shape, y_all.dtype),
                   jax.ShapeDtypeStruct((c_out[2], nseq, M_HEADS, M_DK, M_DV), F32),
                   jax.ShapeDtypeStruct((nseq, M_HEADS, M_DK), F32),
                   jax.ShapeDtypeStruct((nseq * SUBLANES, LANES), F32)],
        scratch_shapes=[pltpu.VMEM((R, M_WIDTH), F32), pltpu.VMEM((5, M_HEADS, R, LANES), F32)],
        compiler_params=_params(("parallel", "arbitrary")),
        input_output_aliases=aliases,
        name="mlstm",
    )(*passthrough, pm, pm, pm, pm, pg, bias_row, norm_row, c_in[0], n0, m0)


GDN_HEAD_GROUP = 16


def _gdn_kernel(nb, L, n_passthrough, *refs):
    (q_ref, k_ref, v_ref, z_ref, g_ref, cw_ref, hist0_ref, alog_ref, dtb_ref, nw_ref, s0_ref,
     y_ref, s_ref, hist_ref, k_s, kq_s, vb_s, t_s, at_s, ws_s, vn_s, g_s) = refs[n_passthrough:]
    R = nb * L

    @pl.when(pl.program_id(1) == 0)
    def _():
        s_ref[...] = s0_ref[...]
        hist_ref[...] = hist0_ref[...]

    zt = g_ref[...]
    beta = pltpu.roll(jax.nn.sigmoid(zt), LANES - GATE_GB, 1)
    gdec = pltpu.roll(-jnp.exp(alog_ref[...]) * _softplus(zt + dtb_ref[...]), LANES - GATE_GA, 1)
    causal, strict = _seq_masks(nb, L)
    gcum = _cumsum_rows(gdec, nb, L)
    g_s[...] = gcum
    grow = gcum.T

    eye = (lax.broadcasted_iota(jnp.int32, (R, R), 0) == lax.broadcasted_iota(jnp.int32, (R, R), 1)).astype(F32)
    levels = int(math.log2(L)) - 1
    for h0 in range(0, G_HEADS, GDN_HEAD_GROUP):
        heads = range(h0, h0 + GDN_HEAD_GROUP)
        lhs, ks, decays = [], [], []
        for h in heads:
            sl = slice(h * G_DK, (h + 1) * G_DK)
            parts = []
            for idx, ref in enumerate((q_ref, k_ref, v_ref)):
                cs = slice(idx * G_WIDTH + h * G_DK, idx * G_WIDTH + (h + 1) * G_DK)
                parts.append(_silu(_conv4(ref[:, sl], hist_ref[:, cs], cw_ref[:, cs], nb, L)))
            qh, kh, vh = parts
            q = qh * lax.rsqrt(jnp.sum(qh * qh, axis=-1, keepdims=True) + NORM_EPS) * (G_DK ** -0.5)
            k = kh * lax.rsqrt(jnp.sum(kh * kh, axis=-1, keepdims=True) + NORM_EPS)
            bc = beta[:, h:h + 1]
            gc = gcum[:, h:h + 1]
            decays.append(jnp.where(causal, jnp.exp(jnp.where(causal, gc - grow[h:h + 1, :], 0.0)), 0.0))
            kb = k * bc
            eg = jnp.exp(gc)
            k_s[h] = k
            kq_s[h, 0] = kb * eg
            kq_s[h, 1] = q * eg
            vb_s[h] = vh * bc
            lhs.append(jnp.concatenate([kb, q], axis=0))
            ks.append(k)
        aas = [_dot_nt(l, k) for l, k in zip(lhs, ks)]
        avs = [jnp.where(strict, aa[:R] * d, 0.0) for aa, d in zip(aas, decays)]
        for h, aa, d in zip(heads, aas, decays):
            at_s[h] = aa[R:] * d
        ps = [eye - a for a in avs]
        bs = [_dot(a, a) for a in avs]
        for lvl in range(levels):
            if lvl + 1 < levels:
                xs = [_dot(b, jnp.concatenate([b, p], axis=1)) for b, p in zip(bs, ps)]
                bs = [x[:, :R] for x in xs]
                ps = [p + x[:, R:] for p, x in zip(ps, xs)]
            else:
                ps = [p + _dot(b, p) for b, p in zip(bs, ps)]
        for h, p in zip(heads, ps):
            t_s[h] = p

    def read_state(s):
        rows = _rows(s, L)
        for h in range(G_HEADS):
            x = _dot(jnp.concatenate([kq_s[h, 0, rows, :], kq_s[h, 1, rows, :]], axis=0), s_ref[s, h])
            ws_s[h, 0, rows, :] = x[:L]
            ws_s[h, 1, rows, :] = x[L:]

    _for_each_seq(nb, read_state)

    for h in range(G_HEADS):
        sl = slice(h * G_DK, (h + 1) * G_DK)
        v_new = _dot(t_s[h], vb_s[h] - ws_s[h, 0])
        vn_s[h] = v_new
        o = ws_s[h, 1] + _dot(at_s[h], v_new)
        on = o * lax.rsqrt(jnp.mean(o * o, axis=-1, keepdims=True) + NORM_EPS) * nw_ref[...]
        y_ref[:, sl] = (on * _silu(z_ref[:, sl])).astype(y_ref.dtype)

    def write_state(s):
        rows = _rows(s, L)
        for h in range(G_HEADS):
            gseq = g_s[rows, h:h + 1]
            g_end = gseq[L - 1:L]
            kdec = k_s[h, rows, :] * jnp.exp(g_end - gseq)
            s_ref[s, h] = s_ref[s, h] * jnp.exp(g_end) + _dot_tn(kdec, vn_s[h, rows, :])

    _for_each_seq(nb, write_state)

    for idx, ref in enumerate((q_ref, k_ref, v_ref)):
        hist_ref[:, idx * G_WIDTH:(idx + 1) * G_WIDTH] = _next_hist(ref[...], nb, L)


def _gdn_group(pm, pg, y_all, row0, nseq, nb, L, nch, s_in, hist0, prm, s_out):
    R = nb * L
    base = row0 // R
    nsb = nseq // nb
    rb = lambda sb, c: base + sb * nch + c
    full = lambda shape: pl.BlockSpec(shape, lambda sb, c: (0,) * len(shape))
    sblock = (nb, G_HEADS, G_DK, G_DV)
    hst = pl.BlockSpec((nb * SUBLANES, G_QKV), lambda sb, c: (sb, 0))
    col = lambda off: pl.BlockSpec((R, G_WIDTH), lambda sb, c: (rb(sb, c), off // G_WIDTH))
    passthrough, aliases = _alias_inputs(y_all, s_out)
    return pl.pallas_call(
        functools.partial(_gdn_kernel, nb, L, len(passthrough)),
        grid=(nsb, nch),
        in_specs=[pl.BlockSpec(memory_space=pl.ANY)] * len(passthrough) + [
                  col(COL_GQ), col(COL_GK), col(COL_GV), col(COL_GZ),
                  pl.BlockSpec((R, LANES), lambda sb, c: (rb(sb, c), 0)),
                  full((CONV_W, G_QKV)), hst, full((1, LANES)), full((1, LANES)), full((1, G_DV)),
                  _stacked_state(s_in[1], sblock)],
        out_specs=[pl.BlockSpec((R, G_WIDTH), lambda sb, c: (rb(sb, c), 0)),
                   _stacked_state(s_out[1], sblock), hst],
        out_shape=[jax.ShapeDtypeStruct(y_all.shape, y_all.dtype),
                   jax.ShapeDtypeStruct((s_out[2], nseq, G_HEADS, G_DK, G_DV), F32),
                   jax.ShapeDtypeStruct((nseq * SUBLANES, G_QKV), F32)],
        scratch_shapes=[pltpu.VMEM((G_HEADS, R, G_DK), F32), pltpu.VMEM((G_HEADS, 2, R, G_DK), F32),
                        pltpu.VMEM((G_HEADS, R, G_DV), F32), pltpu.VMEM((G_HEADS, R, R), F32),
                        pltpu.VMEM((G_HEADS, R, R), F32), pltpu.VMEM((G_HEADS, 2, R, G_DV), F32),
                        pltpu.VMEM((G_HEADS, R, G_DV), F32), pltpu.VMEM((R, LANES), F32)],
        compiler_params=_params(("parallel", "arbitrary")),
        input_output_aliases=aliases,
        name="gdn",
    )(*passthrough, pm, pm, pm, pm, pg, prm[0], hist0, prm[1], prm[2], prm[3], s_in[0])


def _lane_row(vec, offset):
    return jnp.zeros((1, LANES), F32).at[0, offset:offset + vec.shape[0]].set(vec.astype(F32))


def _pad_hist(buf):
    n, w, c = buf.shape
    return jnp.pad(buf.astype(F32), ((0, 0), (0, SUBLANES - w), (0, 0))).reshape(n * SUBLANES, c)


def _unpad_hist(hist, n):
    return hist.reshape(n, SUBLANES, hist.shape[-1])[:, :CONV_W - 1]


def _mixers(proj, mix, group, states, big_out, lw):
    pa, ps, pg = proj
    row0, nseq, nb_m, nb_r, nb_g, L_m, L_r, L_g, T = group
    c_in, n0, m0, h0, rhist, s_in, ghist = states
    ym, c, n, m = _mlstm_group(pa, pg, mix[0], row0, nseq, nb_m, L_m, T // L_m, c_in, n0, m0,
                               lw["m_bias"], lw["m_norm"], big_out[0])
    yr, h, rh = _rglru_group(ps, mix[1], row0, nseq, nb_r, L_r, T // L_r, rhist, h0, lw["r_prm"])
    yg, s, gh = _gdn_group(ps, pg, mix[2], row0, nseq, nb_g, L_g, T // L_g, s_in, ghist, lw["g_prm"], big_out[1])
    return (ym, yr, yg), (c, n, m, h, rh, s, gh)


def kernel(x_prompt, x_sample, state_mlstm_C, state_mlstm_n, state_mlstm_m, state_rglru_h, state_rglru_conv, state_gdn_S, state_gdn_conv, meta_tokens, norm_mix, w_in, m_bias_i, m_bias_f, m_norm, r_conv_w, r_conv_b, r_gate_a_w, r_gate_a_b, r_gate_x_w, r_gate_x_b, r_lambda, g_conv_w, g_A_log, g_dt_bias, g_norm, w_out, norm_ffn, w_gate, w_up, w_down, norm_final):
    batch, seq, d = x_prompt.shape
    dec_batch, dec_seq, _ = x_sample.shape
    depth = w_in.shape[0]
    n_prompt = batch * seq
    n_sample = dec_batch * dec_seq
    n_rows = n_prompt + n_sample + batch * N_META

    meta = jnp.broadcast_to(meta_tokens.astype(F32)[None], (batch, N_META, d))
    x = jnp.concatenate([x_prompt.reshape(n_prompt, d), x_sample.reshape(n_sample, d),
                         meta.reshape(batch * N_META, d)], axis=0)

    g_meta = (n_prompt + n_sample, batch, batch, batch, batch, N_META, N_META, N_META, N_META)
    g_prompt = (0, batch, 1, 1, 1, 256, 256, 64, seq)
    g_sample = (n_prompt, dec_batch, 16, 32, 8, dec_seq, dec_seq, dec_seq, dec_seq)

    zeros = lambda *shape: jnp.zeros(shape, F32)
    meta_states = ((zeros(1, batch, M_HEADS, M_DK, M_DV), 0), zeros(batch, M_HEADS, M_DK),
                   zeros(batch * SUBLANES, LANES), zeros(batch * SUBLANES, R_WIDTH), zeros(batch * SUBLANES, R_WIDTH),
                   (zeros(1, batch, G_HEADS, G_DK, G_DV), 0), zeros(batch * SUBLANES, G_QKV))
    one_layer = ((None, 0, 1), (None, 0, 1))
    w_down_bf16 = w_down.astype(BF16)
    w_in_t = jnp.swapaxes(w_in, 1, 2)

    prompt_new, sample_new = [], []
    sample_c = sample_s = None
    for l in range(depth):
        lw = dict(
            m_bias=_lane_row(m_bias_i[l], GATE_MI) + _lane_row(m_bias_f[l], GATE_MF),
            m_norm=m_norm[l].reshape(1, M_WIDTH),
            r_prm=(r_conv_w[l], r_conv_b[l].reshape(1, R_WIDTH), r_gate_a_w[l].astype(BF16),
                   r_gate_a_b[l].reshape(1, R_WIDTH), r_gate_x_w[l].astype(BF16),
                   r_gate_x_b[l].reshape(1, R_WIDTH), r_lambda[l].reshape(1, R_WIDTH)),
            g_prm=(g_conv_w[l], _lane_row(g_A_log[l], GATE_GA), _lane_row(g_dt_bias[l], GATE_GA),
                   g_norm[l].reshape(1, G_DV)),
        )
        sample_states = ((state_mlstm_C, l), state_mlstm_n[l],
                         jnp.pad(jnp.repeat(state_mlstm_m[l], SUBLANES, axis=0), ((0, 0), (0, LANES - M_HEADS))),
                         jnp.repeat(state_rglru_h[l], SUBLANES, axis=0), _pad_hist(state_rglru_conv[l]),
                         (state_gdn_S, l), _pad_hist(state_gdn_conv[l]))

        hn = _rmsnorm(x, norm_mix[l], BF16)
        proj = _in_proj(hn, w_in_t, l)

        def as_input(st):
            return ((st[0], 0),) + st[1:5] + ((st[5], 0), st[6])

        mix = tuple(jnp.zeros((n_rows, width), BF16) for width in (M_WIDTH, R_WIDTH, G_WIDTH))
        mix, st_meta = _mixers(proj, mix, g_meta, meta_states, one_layer, lw)
        mix, st_prompt = _mixers(proj, mix, g_prompt, as_input(st_meta), one_layer, lw)
        mix, st_sample = _mixers(proj, mix, g_sample, sample_states,
                                 ((sample_c, l, depth), (sample_s, l, depth)), lw)
        sample_c, sample_s = st_sample[0], st_sample[5]
        prompt_new.append(st_prompt)
        sample_new.append(st_sample)

        x = _matmul_resid(mix, w_out, l, x, ROWS_DENSE, COLS_DENSE, "out_proj")
        hf = _rmsnorm(x, norm_ffn[l], BF16)
        hmid = _matmul_swiglu(hf, w_gate, w_up, l)
        x = _matmul_resid([hmid], w_down_bf16, l, x, ROWS_DOWN, COLS_DOWN, "ffn_down")

    y_prompt_out, y_sample_out = _final_norm(x, norm_final, n_prompt, n_sample)
    y_prompt_out = y_prompt_out.reshape(batch, seq, d)
    y_sample_out = y_sample_out.reshape(dec_batch, dec_seq, d)

    def small_states(per_layer, n):
        nn, m, h, rh, gh = (jnp.stack([st[i] for st in per_layer]) for i in (1, 2, 3, 4, 6))
        return (nn, m[:, ::SUBLANES, :M_HEADS], h[:, ::SUBLANES],
                jax.vmap(lambda a: _unpad_hist(a, n))(rh), jax.vmap(lambda a: _unpad_hist(a, n))(gh))

    pn, pm_, ph, prh, pgh = small_states(prompt_new, batch)
    sn, sm, sh, srh, sgh = small_states(sample_new, dec_batch)
    prompt_c = jnp.concatenate([st[0] for st in prompt_new], axis=0)
    prompt_s = jnp.concatenate([st[5] for st in prompt_new], axis=0)
    return (y_prompt_out, y_sample_out, prompt_c, pn, pm_, ph, prh, prompt_s, pgh,
            sample_c, sn, sm, sh, srh, sample_s, sgh)
```

```python
import functools
import math

import jax
import jax.numpy as jnp
from jax import lax
from jax.experimental import pallas as pl
from jax.experimental.pallas import tpu as pltpu

F32 = jnp.float32
BF16 = jnp.bfloat16

N_META = 16
CONV_W = 4
NORM_EPS = 1e-6
M_HEADS = 4
M_DV = 256
M_DK = 128
M_WIDTH = M_HEADS * M_DV
R_WIDTH = 1024
R_BLOCKS = 8
R_BDIM = R_WIDTH // R_BLOCKS
R_C = 8.0
G_DK = 128
G_DV = 128
G_HEADS = 16
G_WIDTH = G_HEADS * G_DV
G_QKV = 3 * G_WIDTH

LANES = 128
SUBLANES = 8
VMEM_LIMIT = 56 * 2 ** 20

SRC_GATES_M = 3072
SRC_S = 3080
SRC_SHIFT = SRC_S % LANES
N_A = 3072
N_S = 10240
SRC_GATES_G = SRC_S + N_S - SRC_SHIFT
N_IN = SRC_S + N_S + 32
COL_MQ, COL_MK, COL_MV, COL_MO = 0, 512, 1024, 2048
COL_RX, COL_RG, COL_GQ, COL_GK, COL_GV, COL_GZ = 0, 1024, 2048, 4096, 6144, 8192
GATE_MI, GATE_MF, GATE_GB, GATE_GA = 0, 4, 8, 24
N_GATES = 40

ROWS_DENSE = 1856
ROWS_DOWN = 464
COLS_DENSE = 256
COLS_DOWN = 512
ROWS_NORM = 464
ROWS_FINAL_NORM = 512


def _params(semantics):
    return pltpu.CompilerParams(dimension_semantics=semantics, vmem_limit_bytes=VMEM_LIMIT)


def _dot(a, b):
    return jnp.dot(a.astype(BF16), b.astype(BF16), preferred_element_type=F32)


def _dot_nt(a, b):
    return lax.dot_general(a.astype(BF16), b.astype(BF16), (((1,), (1,)), ((), ())),
                           preferred_element_type=F32)


def _dot_tn(a, b):
    return lax.dot_general(a.astype(BF16), b.astype(BF16), (((0,), (0,)), ((), ())),
                           preferred_element_type=F32)


def _softplus(x):
    return jnp.maximum(x, 0.0) + jnp.log1p(jnp.exp(-jnp.abs(x)))


def _silu(x):
    return x * jax.nn.sigmoid(x)


def _seq_masks(nb, L):
    R = nb * L
    row = lax.broadcasted_iota(jnp.int32, (R, R), 0)
    col = lax.broadcasted_iota(jnp.int32, (R, R), 1)
    causal = row >= col
    strict = row > col
    if nb > 1:
        same = (row // L) == (col // L)
        causal = jnp.logical_and(causal, same)
        strict = jnp.logical_and(strict, same)
    return causal, strict


def _cumsum_rows(x, nb, L):
    tpos = lax.broadcasted_iota(jnp.int32, (nb * L, 1), 0) % L
    step = 1
    while step < L:
        x = x + jnp.where(tpos >= step, pltpu.roll(x, step, 0), 0.0)
        step *= 2
    return x


def _conv4(x, hist, w, nb, L):
    R = nb * L
    acc = x * w[CONV_W - 1:CONV_W, :]
    if L == SUBLANES:
        tpos = lax.broadcasted_iota(jnp.int32, (R, 1), 0) % L
    else:
        tpos = lax.broadcasted_iota(jnp.int32, (SUBLANES, 1), 0)
    for j in range(1, CONV_W):
        xs = pltpu.roll(x, j, 0)
        shift = (j - (CONV_W - 1)) % (nb * SUBLANES)
        hr = hist if shift == 0 else pltpu.roll(hist, shift, 0)
        if L == SUBLANES:
            sh = jnp.where(tpos < j, hr, xs)
        else:
            parts = []
            for s in range(nb):
                head = jnp.where(tpos < j, hr[s * SUBLANES:(s + 1) * SUBLANES],
                                 xs[s * L:s * L + SUBLANES])
                parts += [head, xs[s * L + SUBLANES:(s + 1) * L]]
            sh = jnp.concatenate(parts, axis=0)
        acc = acc + sh * w[CONV_W - 1 - j:CONV_W - j, :]
    return acc


def _next_hist(x, nb, L):
    if L == SUBLANES:
        return pltpu.roll(x, nb * L - (SUBLANES - (CONV_W - 1)), 0)
    parts = [pltpu.roll(x[(s + 1) * L - SUBLANES:(s + 1) * L], CONV_W - 1, 0) for s in range(nb)]
    return parts[0] if nb == 1 else jnp.concatenate(parts, axis=0)


def _for_each_seq(nb, fn):
    if nb <= 4:
        for s in range(nb):
            fn(s)
    else:
        def body(s, carry):
            fn(s)
            return carry
        lax.fori_loop(0, nb, body, 0)


def _rows(s, L):
    start = s * L
    if not isinstance(start, int):
        start = pl.multiple_of(start, SUBLANES)
    return pl.ds(start, L)


def _rmsnorm_kernel(x_ref, w_ref, o_ref):
    x = x_ref[...]
    y = x * lax.rsqrt(jnp.mean(x * x, axis=-1, keepdims=True) + NORM_EPS)
    o_ref[...] = (y * w_ref[...]).astype(o_ref.dtype)


def _rmsnorm(x, w, out_dtype, tm=ROWS_NORM):
    n, d = x.shape
    return pl.pallas_call(
        _rmsnorm_kernel,
        grid=(n // tm,),
        in_specs=[pl.BlockSpec((tm, d), lambda i: (i, 0)), pl.BlockSpec((1, d), lambda i: (0, 0))],
        out_specs=pl.BlockSpec((tm, d), lambda i: (i, 0)),
        out_shape=jax.ShapeDtypeStruct((n, d), out_dtype),
        compiler_params=_params(("parallel",)),
        name="rmsnorm",
    )(x, w.reshape(1, d))


def _final_norm_kernel(n_first, x_ref, w_ref, a_ref, b_ref):
    x = x_ref[...]
    y = x * lax.rsqrt(jnp.mean(x * x, axis=-1, keepdims=True) + NORM_EPS) * w_ref[...]
    i = pl.program_id(0)

    @pl.when(i < n_first)
    def _():
        a_ref[...] = y

    @pl.when(i >= n_first)
    def _():
        b_ref[...] = y


def _final_norm(x, w, rows_a, rows_b, tm=ROWS_FINAL_NORM):
    d = x.shape[1]
    na, nb = rows_a // tm, rows_b // tm
    return pl.pallas_call(
        functools.partial(_final_norm_kernel, na),
        grid=(na + nb,),
        in_specs=[pl.BlockSpec((tm, d), lambda i: (i, 0)), pl.BlockSpec((1, d), lambda i: (0, 0))],
        out_specs=[pl.BlockSpec((tm, d), lambda i: (jnp.minimum(i, na - 1), 0)),
                   pl.BlockSpec((tm, d), lambda i: (jnp.maximum(i - na, 0), 0))],
        out_shape=[jax.ShapeDtypeStruct((rows_a, d), F32), jax.ShapeDtypeStruct((rows_b, d), F32)],
        compiler_params=_params(("arbitrary",)),
        name="final_norm",
    )(x, w.reshape(1, d))


def _layer_cols(layer, k, tn, col_block):
    return pl.BlockSpec((None, k, tn), lambda i, j: (layer, 0, col_block(j)))


def _layer_rows(layer, rows, k, row_block):
    return pl.BlockSpec((None, rows, k), lambda i, j: (layer, row_block(j), 0))


def _dot_wt(a, wt):
    return lax.dot_general(a, wt.astype(BF16), (((1,), (1,)), ((), ())), preferred_element_type=F32)


def _in_proj_a_kernel(a_ref, wt_ref, o_ref):
    o_ref[...] = _dot_wt(a_ref[...], wt_ref[...])


def _in_proj_s_kernel(a_ref, wt_ref, wnext_ref, o_ref):
    o_ref[...] = _dot_wt(a_ref[...], jnp.concatenate([wt_ref[SRC_SHIFT:, :], wnext_ref[...]], axis=0))


def _in_proj_gates_kernel(a_ref, wm_ref, g0_ref, g1_ref, g2_ref, g3_ref, o_ref):
    k = wm_ref.shape[1]
    wt = jnp.concatenate([wm_ref[...], g0_ref[...], g1_ref[...], g2_ref[...], g3_ref[...],
                          jnp.zeros((LANES - N_GATES, k), F32)], axis=0)
    o_ref[...] = _dot_wt(a_ref[...], wt)


def _in_proj(hn, w_in_t, layer, tm=ROWS_DENSE, tn=COLS_DENSE):
    m, k = hn.shape
    assert SRC_SHIFT == SUBLANES and w_in_t.shape[1:] == (N_IN, k)
    a_spec = pl.BlockSpec((tm, k), lambda i, j: (i, 0))
    out_spec = pl.BlockSpec((tm, tn), lambda i, j: (i, j))
    proj_a = pl.pallas_call(
        _in_proj_a_kernel,
        grid=(m // tm, N_A // tn),
        in_specs=[a_spec, _layer_rows(layer, tn, k, lambda j: j)],
        out_specs=out_spec,
        out_shape=jax.ShapeDtypeStruct((m, N_A), F32),
        compiler_params=_params(("parallel", "arbitrary")),
        name="in_proj_a",
    )(hn, w_in_t)
    first = (SRC_S - SRC_SHIFT) // tn
    proj_s = pl.pallas_call(
        _in_proj_s_kernel,
        grid=(m // tm, N_S // tn),
        in_specs=[a_spec, _layer_rows(layer, tn, k, lambda j: first + j),
                  _layer_rows(layer, SUBLANES, k, lambda j: (first + j + 1) * (tn // SUBLANES))],
        out_specs=out_spec,
        out_shape=jax.ShapeDtypeStruct((m, N_S), F32),
        compiler_params=_params(("parallel", "arbitrary")),
        name="in_proj_s",
    )(hn, w_in_t, w_in_t)
    gate_rows = [SRC_GATES_M] + [SRC_GATES_G + SRC_SHIFT + r * SUBLANES for r in range(4)]
    gates = pl.pallas_call(
        _in_proj_gates_kernel,
        grid=(m // tm, 1),
        in_specs=[a_spec] + [_layer_rows(layer, SUBLANES, k, lambda j, r=r: r // SUBLANES) for r in gate_rows],
        out_specs=pl.BlockSpec((tm, LANES), lambda i, j: (i, 0)),
        out_shape=jax.ShapeDtypeStruct((m, LANES), F32),
        compiler_params=_params(("parallel", "arbitrary")),
        name="in_proj_gates",
    )(hn, *([w_in_t] * len(gate_rows)))
    return proj_a, proj_s, gates


def _mm_swiglu_kernel(a_ref, wg_ref, wu_ref, o_ref):
    a = a_ref[...]
    g = jnp.dot(a, wg_ref[...].astype(BF16), preferred_element_type=F32)
    u = jnp.dot(a, wu_ref[...].astype(BF16), preferred_element_type=F32)
    o_ref[...] = (_silu(g) * u).astype(o_ref.dtype)


def _matmul_swiglu(a, wg, wu, layer, tm=ROWS_DENSE, tn=COLS_DENSE):
    m, k = a.shape
    n = wg.shape[2]
    return pl.pallas_call(
        _mm_swiglu_kernel,
        grid=(m // tm, n // tn),
        in_specs=[pl.BlockSpec((tm, k), lambda i, j: (i, 0)),
                  _layer_cols(layer, k, tn, lambda j: j), _layer_cols(layer, k, tn, lambda j: j)],
        out_specs=pl.BlockSpec((tm, tn), lambda i, j: (i, j)),
        out_shape=jax.ShapeDtypeStruct((m, n), BF16),
        compiler_params=_params(("parallel", "arbitrary")),
        name="ffn_gate_up",
    )(a, wg, wu)


def _mm_resid_kernel(widths, *refs):
    a_refs = refs[:len(widths)]
    w_ref, x_ref, o_ref = refs[len(widths):]
    acc = x_ref[...]
    k0 = 0
    for a_ref, kw in zip(a_refs, widths):
        acc = acc + jnp.dot(a_ref[...], w_ref[k0:k0 + kw, :].astype(BF16), preferred_element_type=F32)
        k0 += kw
    o_ref[...] = acc


def _matmul_resid(a_list, w, layer, x, tm, tn, name):
    m = x.shape[0]
    _, k, n = w.shape
    widths = tuple(a.shape[1] for a in a_list)
    return pl.pallas_call(
        functools.partial(_mm_resid_kernel, widths),
        grid=(m // tm, n // tn),
        in_specs=[pl.BlockSpec((tm, kw), lambda i, j: (i, 0)) for kw in widths]
        + [_layer_cols(layer, k, tn, lambda j: j), pl.BlockSpec((tm, tn), lambda i, j: (i, j))],
        out_specs=pl.BlockSpec((tm, tn), lambda i, j: (i, j)),
        out_shape=jax.ShapeDtypeStruct((m, n), F32),
        compiler_params=_params(("parallel", "arbitrary")),
        name=name,
    )(*a_list, w, x)


def _rglru_kernel(nb, L, _y_all_ref, rx_ref, rg_ref, hist0_ref, h0_ref, cw_ref, cb_ref, wa_ref, ba_ref, wx_ref,
                  bx_ref, lam_ref, y_ref, h_ref, hist_ref, a_s, u_s):
    @pl.when(pl.program_id(1) == 0)
    def _():
        h_ref[...] = h0_ref[...]
        hist_ref[...] = hist0_ref[...]

    x = rx_ref[...]
    xc = _conv4(x, hist_ref[...], cw_ref[...], nb, L) + cb_ref[...]
    hist_ref[...] = _next_hist(x, nb, L)

    xcb = xc.astype(BF16)
    ga = jnp.concatenate([jnp.dot(xcb[:, n * R_BDIM:(n + 1) * R_BDIM], wa_ref[n], preferred_element_type=F32)
                          for n in range(R_BLOCKS)], axis=1)
    gx = jnp.concatenate([jnp.dot(xcb[:, n * R_BDIM:(n + 1) * R_BDIM], wx_ref[n], preferred_element_type=F32)
                          for n in range(R_BLOCKS)], axis=1)
    r = jax.nn.sigmoid(ga + ba_ref[...])
    i = jax.nn.sigmoid(gx + bx_ref[...])
    log_a = -R_C * r * _softplus(-lam_ref[...])
    a_s[...] = jnp.exp(log_a)
    th = jnp.tanh(log_a)
    u_s[...] = jnp.sqrt(-2.0 * th / (1.0 - th)) * (i * xc)

    if nb == 1:
        def step(t, h):
            h = a_s[pl.ds(t, 1), :] * h + u_s[pl.ds(t, 1), :]
            u_s[pl.ds(t, 1), :] = h
            return h

        h_last = lax.fori_loop(0, L, step, h_ref[0:1, :], unroll=SUBLANES)
        h_ref[...] = jnp.broadcast_to(h_last, (SUBLANES, R_WIDTH))
        hs = u_s[...]
    else:
        a, u = a_s[...], u_s[...]
        tpos = lax.broadcasted_iota(jnp.int32, (nb * L, 1), 0) % L
        step = 1
        while step < L:
            valid = tpos >= step
            u = jnp.where(valid, a * pltpu.roll(u, step, 0) + u, u)
            a = jnp.where(valid, a * pltpu.roll(a, step, 0), a)
            step *= 2
        if L == SUBLANES:
            h0 = h_ref[...]
        else:
            h0 = jnp.concatenate([jnp.broadcast_to(h_ref[s * SUBLANES:s * SUBLANES + 1, :], (L, R_WIDTH))
                                  for s in range(nb)], axis=0)
        hs = a * h0 + u
        if L == SUBLANES:
            h_ref[...] = pltpu.roll(hs, nb * L - (L - 1), 0)
        else:
            h_ref[...] = jnp.concatenate([jnp.broadcast_to(hs[(s + 1) * L - 1:(s + 1) * L, :], (SUBLANES, R_WIDTH))
                                          for s in range(nb)], axis=0)
    y_ref[...] = (hs * jax.nn.gelu(rg_ref[...])).astype(y_ref.dtype)


def _rglru_group(pm, y_all, row0, nseq, nb, L, nch, hist0, h0, prm):
    R = nb * L
    base = row0 // R
    nsb = nseq // nb
    rb = lambda sb, c: base + sb * nch + c
    full = lambda shape: pl.BlockSpec(shape, lambda sb, c: (0,) * len(shape))
    st = pl.BlockSpec((nb * SUBLANES, R_WIDTH), lambda sb, c: (sb, 0))
    return pl.pallas_call(
        functools.partial(_rglru_kernel, nb, L),
        grid=(nsb, nch),
        in_specs=[pl.BlockSpec(memory_space=pl.ANY),
                  pl.BlockSpec((R, R_WIDTH), lambda sb, c: (rb(sb, c), COL_RX // R_WIDTH)),
                  pl.BlockSpec((R, R_WIDTH), lambda sb, c: (rb(sb, c), COL_RG // R_WIDTH)),
                  st, st,
                  full((CONV_W, R_WIDTH)), full((1, R_WIDTH)),
                  full((R_BLOCKS, R_BDIM, R_BDIM)), full((1, R_WIDTH)),
                  full((R_BLOCKS, R_BDIM, R_BDIM)), full((1, R_WIDTH)), full((1, R_WIDTH))],
        out_specs=[pl.BlockSpec((R, R_WIDTH), lambda sb, c: (rb(sb, c), 0)), st, st],
        out_shape=[jax.ShapeDtypeStruct(y_all.shape, y_all.dtype),
                   jax.ShapeDtypeStruct((nseq * SUBLANES, R_WIDTH), F32),
                   jax.ShapeDtypeStruct((nseq * SUBLANES, R_WIDTH), F32)],
        scratch_shapes=[pltpu.VMEM((R, R_WIDTH), F32), pltpu.VMEM((R, R_WIDTH), F32)],
        compiler_params=_params(("parallel", "arbitrary")),
        input_output_aliases={0: 0},
        name="rglru",
    )(y_all, pm, pm, hist0, h0, *prm)


def _mlstm_kernel(nb, L, n_passthrough, *refs):
    (q_ref, k_ref, v_ref, o_ref, g_ref, bias_ref, nw_ref, c0_ref, n0_ref, m0_ref,
     y_ref, c_ref, n_ref, m_ref, num_s, col_s) = refs[n_passthrough:]
    R = nb * L

    @pl.when(pl.program_id(1) == 0)
    def _():
        c_ref[...] = c0_ref[...]
        n_ref[...] = n0_ref[...]
        m_ref[...] = m0_ref[...]

    z = g_ref[...] + bias_ref[...]
    lf = pltpu.roll(-_softplus(-z), LANES - GATE_MF, 1)
    causal, _ = _seq_masks(nb, L)
    fcum = _cumsum_rows(lf, nb, L)
    bcol = z - fcum
    brow = bcol.T
    if L == SUBLANES:
        m_rows = m_ref[...]
    else:
        m_rows = jnp.concatenate(
            [jnp.broadcast_to(m_ref[s * SUBLANES:s * SUBLANES + 1, :], (L, LANES)) for s in range(nb)], axis=0)

    for h in range(M_HEADS):
        fc = fcum[:, h:h + 1]
        d = fc + brow[h:h + 1, :]
        rowmax = jnp.max(jnp.where(causal, d, -jnp.inf), axis=-1, keepdims=True)
        inter = m_rows[:, h:h + 1] + fc
        mt = jnp.maximum(rowmax, inter)
        p = jnp.where(causal, jnp.exp(jnp.where(causal, d - mt, 0.0)), 0.0)
        qh = q_ref[:, h * M_DK:(h + 1) * M_DK] * (M_DK ** -0.5)
        s = _dot_nt(qh, k_ref[:, h * M_DK:(h + 1) * M_DK]) * p
        num_s[:, h * M_DV:(h + 1) * M_DV] = _dot(s, v_ref[:, h * M_DV:(h + 1) * M_DV])
        for idx, val in enumerate((mt, jnp.exp(inter - mt), jnp.sum(s, axis=-1, keepdims=True),
                                   bcol[:, h:h + 1], fc)):
            col_s[idx, h] = jnp.broadcast_to(val, (R, LANES))

    def per_seq(s):
        rows = _rows(s, L)
        mrow = pl.ds(s * SUBLANES, 1)
        for h in range(M_HEADS):
            dk = slice(h * M_DK, (h + 1) * M_DK)
            dv = slice(h * M_DV, (h + 1) * M_DV)
            mt = col_s[0, h, rows, 0:1]
            w_inter = col_s[1, h, rows, 0:1]
            den_in = col_s[2, h, rows, 0:1]
            bc = col_s[3, h, rows, 0:1]
            f_last = col_s[4, h, rows, 0:1][L - 1:L]
            qh = q_ref[rows, dk] * (M_DK ** -0.5)
            kh = k_ref[rows, dk]
            vh = v_ref[rows, dv]
            c_old = c_ref[s, h]
            n_old = n_ref[s, h:h + 1, :]
            m_old = m_ref[mrow, h:h + 1]
            num = w_inter * _dot(qh, c_old) + num_s[rows, dv]
            den = w_inter * jnp.sum(qh * n_old, axis=-1, keepdims=True) + den_in
            hh = num / jnp.maximum(jnp.abs(den), jnp.exp(-mt))
            hn = hh * lax.rsqrt(jnp.mean(hh * hh, axis=-1, keepdims=True) + NORM_EPS) * nw_ref[:, dv]
            y_ref[rows, dv] = (hn * jax.nn.sigmoid(o_ref[rows, dv])).astype(y_ref.dtype)
            m_new = mt[L - 1:L]
            kw = kh * jnp.exp(bc + (f_last - m_new))
            scale = jnp.exp(m_old + f_last - m_new)
            c_ref[s, h] = scale * c_old + _dot_tn(kw, vh)
            n_ref[s, h:h + 1, :] = scale * n_old + jnp.sum(kw, axis=0, keepdims=True)
            m_ref[pl.ds(s * SUBLANES, SUBLANES), h:h + 1] = jnp.broadcast_to(m_new, (SUBLANES, 1))

    _for_each_seq(nb, per_seq)


def _stacked_state(layer, block):
    zeros = (0,) * (len(block) - 1)
    return pl.BlockSpec((None,) + block, lambda sb, c: (layer, sb) + zeros)


def _alias_inputs(y_all, state_out):
    prev, _, _ = state_out
    operands = [y_all] + ([prev] if prev is not None else [])
    return operands, {i: i for i in range(len(operands))}


def _mlstm_group(pm, pg, y_all, row0, nseq, nb, L, nch, c_in, n0, m0, bias_row, norm_row, c_out):
    R = nb * L
    base = row0 // R
    nsb = nseq // nb
    rb = lambda sb, c: base + sb * nch + c
    full = lambda shape: pl.BlockSpec(shape, lambda sb, c: (0,) * len(shape))
    cblock = (nb, M_HEADS, M_DK, M_DV)
    nst = pl.BlockSpec((nb, M_HEADS, M_DK), lambda sb, c: (sb, 0, 0))
    mst = pl.BlockSpec((nb * SUBLANES, LANES), lambda sb, c: (sb, 0))
    passthrough, aliases = _alias_inputs(y_all, c_out)
    return pl.pallas_call(
        functools.partial(_mlstm_kernel, nb, L, len(passthrough)),
        grid=(nsb, nch),
        in_specs=[pl.BlockSpec(memory_space=pl.ANY)] * len(passthrough) + [
                  pl.BlockSpec((R, M_HEADS * M_DK), lambda sb, c: (rb(sb, c), COL_MQ // (M_HEADS * M_DK))),
                  pl.BlockSpec((R, M_HEADS * M_DK), lambda sb, c: (rb(sb, c), COL_MK // (M_HEADS * M_DK))),
                  pl.BlockSpec((R, M_WIDTH), lambda sb, c: (rb(sb, c), COL_MV // M_WIDTH)),
                  pl.BlockSpec((R, M_WIDTH), lambda sb, c: (rb(sb, c), COL_MO // M_WIDTH)),
                  pl.BlockSpec((R, LANES), lambda sb, c: (rb(sb, c), 0)),
                  full((1, LANES)), full((1, M_WIDTH)), _stacked_state(c_in[1], cblock), nst, mst],
        out_specs=[pl.BlockSpec((R, M_WIDTH), lambda sb, c: (rb(sb, c), 0)),
                   _stacked_state(c_out[1], cblock), nst, mst],
        out_shape=[jax.ShapeDtypeStruct(y_all.shape, y_all.dtype),
                   jax.ShapeDtypeStruct((c_out[2], nseq, M_HEADS, M_DK, M_DV), F32),
                   jax.ShapeDtypeStruct((nseq, M_HEADS, M_DK), F32),
                   jax.ShapeDtypeStruct((nseq * SUBLANES, LANES), F32)],
        scratch_shapes=[pltpu.VMEM((R, M_WIDTH), F32), pltpu.VMEM((5, M_HEADS, R, LANES), F32)],
        compiler_params=_params(("parallel", "arbitrary")),
        input_output_aliases=aliases,
        name="mlstm",
    )(*passthrough, pm, pm, pm, pm, pg, bias_row, norm_row, c_in[0], n0, m0)


GDN_HEAD_GROUP = 16
GDN_SOLVE_ROWS = 64


def _gdn_kernel(nb, L, n_passthrough, *refs):
    (q_ref, k_ref, v_ref, z_ref, g_ref, cw_ref, hist0_ref, alog_ref, dtb_ref, nw_ref, s0_ref,
     y_ref, s_ref, hist_ref, k_s, kq_s, vb_s, t_s, at_s, ws_s, vn_s, g_s) = refs[n_passthrough:]
    R = nb * L

    @pl.when(pl.program_id(1) == 0)
    def _():
        s_ref[...] = s0_ref[...]
        hist_ref[...] = hist0_ref[...]

    zt = g_ref[...]
    beta = pltpu.roll(jax.nn.sigmoid(zt), LANES - GATE_GB, 1)
    gdec = pltpu.roll(-jnp.exp(alog_ref[...]) * _softplus(zt + dtb_ref[...]), LANES - GATE_GA, 1)
    causal, strict = _seq_masks(nb, L)
    gcum = _cumsum_rows(gdec, nb, L)
    g_s[...] = gcum
    grow = gcum.T

    eye = (lax.broadcasted_iota(jnp.int32, (R, R), 0) == lax.broadcasted_iota(jnp.int32, (R, R), 1)).astype(F32)
    block = min(L, GDN_SOLVE_ROWS)
    two_blocks = L > block
    assert L in (block, 2 * block)
    levels = int(math.log2(block)) - 1
    if two_blocks:
        diag_blocks = (lax.broadcasted_iota(jnp.int32, (R, R), 0) // block
                       == lax.broadcasted_iota(jnp.int32, (R, R), 1) // block)
    for h0 in range(0, G_HEADS, GDN_HEAD_GROUP):
        heads = range(h0, h0 + GDN_HEAD_GROUP)
        lhs, ks, decays = [], [], []
        for h in heads:
            sl = slice(h * G_DK, (h + 1) * G_DK)
            parts = []
            for idx, ref in enumerate((q_ref, k_ref, v_ref)):
                cs = slice(idx * G_WIDTH + h * G_DK, idx * G_WIDTH + (h + 1) * G_DK)
                parts.append(_silu(_conv4(ref[:, sl], hist_ref[:, cs], cw_ref[:, cs], nb, L)))
            qh, kh, vh = parts
            q = qh * lax.rsqrt(jnp.sum(qh * qh, axis=-1, keepdims=True) + NORM_EPS) * (G_DK ** -0.5)
            k = kh * lax.rsqrt(jnp.sum(kh * kh, axis=-1, keepdims=True) + NORM_EPS)
            bc = beta[:, h:h + 1]
            gc = gcum[:, h:h + 1]
            decays.append(jnp.where(causal, jnp.exp(jnp.where(causal, gc - grow[h:h + 1, :], 0.0)), 0.0))
            kb = k * bc
            eg = jnp.exp(gc)
            k_s[h] = k
            kq_s[h, 0] = kb * eg
            kq_s[h, 1] = q * eg
            vb_s[h] = vh * bc
            lhs.append(jnp.concatenate([kb, q], axis=0))
            ks.append(k)
        aas = [_dot_nt(l, k) for l, k in zip(lhs, ks)]
        avs = [jnp.where(strict, aa[:R] * d, 0.0) for aa, d in zip(aas, decays)]
        for h, aa, d in zip(heads, aas, decays):
            at_s[h] = aa[R:] * d
        if two_blocks:
            offs = [jnp.where(diag_blocks, 0.0, a) for a in avs]
            avs = [jnp.where(diag_blocks, a, 0.0) for a in avs]
        ps = [eye - a for a in avs]
        bs = [_dot(a, a) for a in avs]
        for lvl in range(levels):
            if lvl + 1 < levels:
                xs = [_dot(b, jnp.concatenate([b, p], axis=1)) for b, p in zip(bs, ps)]
                bs = [x[:, :R] for x in xs]
                ps = [p + x[:, R:] for p, x in zip(ps, xs)]
            else:
                ps = [p + _dot(b, p) for b, p in zip(bs, ps)]
        if two_blocks:
            ots = [_dot(o, p) for o, p in zip(offs, ps)]
            ps = [p - _dot(p, ot) for p, ot in zip(ps, ots)]
        for h, p in zip(heads, ps):
            t_s[h] = p

    def read_state(s):
        rows = _rows(s, L)
        for h in range(G_HEADS):
            x = _dot(jnp.concatenate([kq_s[h, 0, rows, :], kq_s[h, 1, rows, :]], axis=0), s_ref[s, h])
            ws_s[h, 0, rows, :] = x[:L]
            ws_s[h, 1, rows, :] = x[L:]

    _for_each_seq(nb, read_state)

    for h in range(G_HEADS):
        sl = slice(h * G_DK, (h + 1) * G_DK)
        v_new = _dot(t_s[h], vb_s[h] - ws_s[h, 0])
        vn_s[h] = v_new
        o = ws_s[h, 1] + _dot(at_s[h], v_new)
        on = o * lax.rsqrt(jnp.mean(o * o, axis=-1, keepdims=True) + NORM_EPS) * nw_ref[...]
        y_ref[:, sl] = (on * _silu(z_ref[:, sl])).astype(y_ref.dtype)

    def write_state(s):
        rows = _rows(s, L)
        for h in range(G_HEADS):
            gseq = g_s[rows, h:h + 1]
            g_end = gseq[L - 1:L]
            kdec = k_s[h, rows, :] * jnp.exp(g_end - gseq)
            s_ref[s, h] = s_ref[s, h] * jnp.exp(g_end) + _dot_tn(kdec, vn_s[h, rows, :])

    _for_each_seq(nb, write_state)

    for idx, ref in enumerate((q_ref, k_ref, v_ref)):
        hist_ref[:, idx * G_WIDTH:(idx + 1) * G_WIDTH] = _next_hist(ref[...], nb, L)


def _gdn_group(pm, pg, y_all, row0, nseq, nb, L, nch, s_in, hist0, prm, s_out):
    R = nb * L
    base = row0 // R
    nsb = nseq // nb
    rb = lambda sb, c: base + sb * nch + c
    full = lambda shape: pl.BlockSpec(shape, lambda sb, c: (0,) * len(shape))
    sblock = (nb, G_HEADS, G_DK, G_DV)
    hst = pl.BlockSpec((nb * SUBLANES, G_QKV), lambda sb, c: (sb, 0))
    col = lambda off: pl.BlockSpec((R, G_WIDTH), lambda sb, c: (rb(sb, c), off // G_WIDTH))
    passthrough, aliases = _alias_inputs(y_all, s_out)
    return pl.pallas_call(
        functools.partial(_gdn_kernel, nb, L, len(passthrough)),
        grid=(nsb, nch),
        in_specs=[pl.BlockSpec(memory_space=pl.ANY)] * len(passthrough) + [
                  col(COL_GQ), col(COL_GK), col(COL_GV), col(COL_GZ),
                  pl.BlockSpec((R, LANES), lambda sb, c: (rb(sb, c), 0)),
                  full((CONV_W, G_QKV)), hst, full((1, LANES)), full((1, LANES)), full((1, G_DV)),
                  _stacked_state(s_in[1], sblock)],
        out_specs=[pl.BlockSpec((R, G_WIDTH), lambda sb, c: (rb(sb, c), 0)),
                   _stacked_state(s_out[1], sblock), hst],
        out_shape=[jax.ShapeDtypeStruct(y_all.shape, y_all.dtype),
                   jax.ShapeDtypeStruct((s_out[2], nseq, G_HEADS, G_DK, G_DV), F32),
                   jax.ShapeDtypeStruct((nseq * SUBLANES, G_QKV), F32)],
        scratch_shapes=[pltpu.VMEM((G_HEADS, R, G_DK), F32), pltpu.VMEM((G_HEADS, 2, R, G_DK), F32),
                        pltpu.VMEM((G_HEADS, R, G_DV), F32), pltpu.VMEM((G_HEADS, R, R), F32),
                        pltpu.VMEM((G_HEADS, R, R), F32), pltpu.VMEM((G_HEADS, 2, R, G_DV), F32),
                        pltpu.VMEM((G_HEADS, R, G_DV), F32), pltpu.VMEM((R, LANES), F32)],
        compiler_params=_params(("parallel", "arbitrary")),
        input_output_aliases=aliases,
        name="gdn",
    )(*passthrough, pm, pm, pm, pm, pg, prm[0], hist0, prm[1], prm[2], prm[3], s_in[0])


def _lane_row(vec, offset):
    return jnp.zeros((1, LANES), F32).at[0, offset:offset + vec.shape[0]].set(vec.astype(F32))


def _pad_hist(buf):
    n, w, c = buf.shape
    return jnp.pad(buf.astype(F32), ((0, 0), (0, SUBLANES - w), (0, 0))).reshape(n * SUBLANES, c)


def _unpad_hist(hist, n):
    return hist.reshape(n, SUBLANES, hist.shape[-1])[:, :CONV_W - 1]


def _mixers(proj, mix, group, states, big_out, lw):
    pa, ps, pg = proj
    row0, nseq, nb_m, nb_r, nb_g, L_m, L_r, L_g, T = group
    c_in, n0, m0, h0, rhist, s_in, ghist = states
    ym, c, n, m = _mlstm_group(pa, pg, mix[0], row0, nseq, nb_m, L_m, T // L_m, c_in, n0, m0,
                               lw["m_bias"], lw["m_norm"], big_out[0])
    yr, h, rh = _rglru_group(ps, mix[1], row0, nseq, nb_r, L_r, T // L_r, rhist, h0, lw["r_prm"])
    yg, s, gh = _gdn_group(ps, pg, mix[2], row0, nseq, nb_g, L_g, T // L_g, s_in, ghist, lw["g_prm"], big_out[1])
    return (ym, yr, yg), (c, n, m, h, rh, s, gh)


def kernel(x_prompt, x_sample, state_mlstm_C, state_mlstm_n, state_mlstm_m, state_rglru_h, state_rglru_conv, state_gdn_S, state_gdn_conv, meta_tokens, norm_mix, w_in, m_bias_i, m_bias_f, m_norm, r_conv_w, r_conv_b, r_gate_a_w, r_gate_a_b, r_gate_x_w, r_gate_x_b, r_lambda, g_conv_w, g_A_log, g_dt_bias, g_norm, w_out, norm_ffn, w_gate, w_up, w_down, norm_final):
    batch, seq, d = x_prompt.shape
    dec_batch, dec_seq, _ = x_sample.shape
    depth = w_in.shape[0]
    n_prompt = batch * seq
    n_sample = dec_batch * dec_seq
    n_rows = n_prompt + n_sample + batch * N_META

    meta = jnp.broadcast_to(meta_tokens.astype(F32)[None], (batch, N_META, d))
    x = jnp.concatenate([x_prompt.reshape(n_prompt, d), x_sample.reshape(n_sample, d),
                         meta.reshape(batch * N_META, d)], axis=0)

    g_meta = (n_prompt + n_sample, batch, batch, batch, batch, N_META, N_META, N_META, N_META)
    g_prompt = (0, batch, 1, 1, 1, 512, 256, 128, seq)
    g_sample = (n_prompt, dec_batch, 16, 32, 8, dec_seq, dec_seq, dec_seq, dec_seq)

    zeros = lambda *shape: jnp.zeros(shape, F32)
    meta_states = ((zeros(1, batch, M_HEADS, M_DK, M_DV), 0), zeros(batch, M_HEADS, M_DK),
                   zeros(batch * SUBLANES, LANES), zeros(batch * SUBLANES, R_WIDTH), zeros(batch * SUBLANES, R_WIDTH),
                   (zeros(1, batch, G_HEADS, G_DK, G_DV), 0), zeros(batch * SUBLANES, G_QKV))
    one_layer = ((None, 0, 1), (None, 0, 1))
    w_down_bf16 = w_down.astype(BF16)
    w_in_t = jnp.swapaxes(w_in, 1, 2)

    prompt_new, sample_new = [], []
    sample_c = sample_s = None
    for l in range(depth):
        lw = dict(
            m_bias=_lane_row(m_bias_i[l], GATE_MI) + _lane_row(m_bias_f[l], GATE_MF),
            m_norm=m_norm[l].reshape(1, M_WIDTH),
            r_prm=(r_conv_w[l], r_conv_b[l].reshape(1, R_WIDTH), r_gate_a_w[l].astype(BF16),
                   r_gate_a_b[l].reshape(1, R_WIDTH), r_gate_x_w[l].astype(BF16),
                   r_gate_x_b[l].reshape(1, R_WIDTH), r_lambda[l].reshape(1, R_WIDTH)),
            g_prm=(g_conv_w[l], _lane_row(g_A_log[l], GATE_GA), _lane_row(g_dt_bias[l], GATE_GA),
                   g_norm[l].reshape(1, G_DV)),
        )
        sample_states = ((state_mlstm_C, l), state_mlstm_n[l],
                         jnp.pad(jnp.repeat(state_mlstm_m[l], SUBLANES, axis=0), ((0, 0), (0, LANES - M_HEADS))),
                         jnp.repeat(state_rglru_h[l], SUBLANES, axis=0), _pad_hist(state_rglru_conv[l]),
                         (state_gdn_S, l), _pad_hist(state_gdn_conv[l]))

        hn = _rmsnorm(x, norm_mix[l], BF16)
        proj = _in_proj(hn, w_in_t, l)

        def as_input(st):
            return ((st[0], 0),) + st[1:5] + ((st[5], 0), st[6])

        mix = tuple(jnp.zeros((n_rows, width), BF16) for width in (M_WIDTH, R_WIDTH, G_WIDTH))
        mix, st_meta = _mixers(proj, mix, g_meta, meta_states, one_layer, lw)
        mix, st_prompt = _mixers(proj, mix, g_prompt, as_input(st_meta), one_layer, lw)
        mix, st_sample = _mixers(proj, mix, g_sample, sample_states,
                                 ((sample_c, l, depth), (sample_s, l, depth)), lw)
        sample_c, sample_s = st_sample[0], st_sample[5]
        prompt_new.append(st_prompt)
        sample_new.append(st_sample)

        x = _matmul_resid(mix, w_out, l, x, ROWS_DENSE, COLS_DENSE, "out_proj")
        hf = _rmsnorm(x, norm_ffn[l], BF16)
        hmid = _matmul_swiglu(hf, w_gate, w_up, l)
        x = _matmul_resid([hmid], w_down_bf16, l, x, ROWS_DOWN, COLS_DOWN, "ffn_down")

    y_prompt_out, y_sample_out = _final_norm(x, norm_final, n_prompt, n_sample)
    y_prompt_out = y_prompt_out.reshape(batch, seq, d)
    y_sample_out = y_sample_out.reshape(dec_batch, dec_seq, d)

    def small_states(per_layer, n):
        nn, m, h, rh, gh = (jnp.stack([st[i] for st in per_layer]) for i in (1, 2, 3, 4, 6))
        return (nn, m[:, ::SUBLANES, :M_HEADS], h[:, ::SUBLANES],
                jax.vmap(lambda a: _unpad_hist(a, n))(rh), jax.vmap(lambda a: _unpad_hist(a, n))(gh))

    pn, pm_, ph, prh, pgh = small_states(prompt_new, batch)
    sn, sm, sh, srh, sgh = small_states(sample_new, dec_batch)
    prompt_c = jnp.concatenate([st[0] for st in prompt_new], axis=0)
    prompt_s = jnp.concatenate([st[5] for st in prompt_new], axis=0)
    return (y_prompt_out, y_sample_out, prompt_c, pn, pm_, ph, prh, prompt_s, pgh,
            sample_c, sn, sm, sh, srh, sample_s, sgh)
```

```python
import functools
import math

import jax
import jax.numpy as jnp
from jax import lax
from jax.experimental import pallas as pl
from jax.experimental.pallas import tpu as pltpu

F32 = jnp.float32
BF16 = jnp.bfloat16

N_META = 16
CONV_W = 4
NORM_EPS = 1e-6
M_HEADS = 4
M_DV = 256
M_DK = 128
M_WIDTH = M_HEADS * M_DV
R_WIDTH = 1024
R_BLOCKS = 8
R_BDIM = R_WIDTH // R_BLOCKS
R_C = 8.0
G_DK = 128
G_DV = 128
G_HEADS = 16
G_WIDTH = G_HEADS * G_DV
G_QKV = 3 * G_WIDTH

LANES = 128
SUBLANES = 8
VMEM_LIMIT = 56 * 2 ** 20

SRC_GATES_M = 3072
SRC_S = 3080
SRC_SHIFT = SRC_S % LANES
N_A = 3072
N_S = 10240
SRC_GATES_G = SRC_S + N_S - SRC_SHIFT
N_IN = SRC_S + N_S + 32
COL_MQ, COL_MK, COL_MV, COL_MO = 0, 512, 1024, 2048
COL_RX, COL_RG, COL_GQ, COL_GK, COL_GV, COL_GZ = 0, 1024, 2048, 4096, 6144, 8192
GATE_MI, GATE_MF, GATE_GB, GATE_GA = 0, 4, 8, 24
N_GATES = 40

ROWS_DENSE = 1856
ROWS_DOWN = 928
COLS_DENSE = 256
COLS_DOWN = 256
ROWS_NORM = 464
ROWS_FINAL_NORM = 512


def _params(semantics):
    return pltpu.CompilerParams(dimension_semantics=semantics, vmem_limit_bytes=VMEM_LIMIT)


def _dot(a, b):
    return jnp.dot(a.astype(BF16), b.astype(BF16), preferred_element_type=F32)


def _dot_nt(a, b):
    return lax.dot_general(a.astype(BF16), b.astype(BF16), (((1,), (1,)), ((), ())),
                           preferred_element_type=F32)


def _dot_tn(a, b):
    return lax.dot_general(a.astype(BF16), b.astype(BF16), (((0,), (0,)), ((), ())),
                           preferred_element_type=F32)


def _softplus(x):
    return jnp.maximum(x, 0.0) + jnp.log1p(jnp.exp(-jnp.abs(x)))


def _silu(x):
    return x * jax.nn.sigmoid(x)


def _seq_masks(nb, L):
    R = nb * L
    row = lax.broadcasted_iota(jnp.int32, (R, R), 0)
    col = lax.broadcasted_iota(jnp.int32, (R, R), 1)
    causal = row >= col
    strict = row > col
    if nb > 1:
        same = (row // L) == (col // L)
        causal = jnp.logical_and(causal, same)
        strict = jnp.logical_and(strict, same)
    return causal, strict


def _cumsum_rows(x, nb, L):
    tpos = lax.broadcasted_iota(jnp.int32, (nb * L, 1), 0) % L
    step = 1
    while step < L:
        x = x + jnp.where(tpos >= step, pltpu.roll(x, step, 0), 0.0)
        step *= 2
    return x


def _conv4(x, hist, w, nb, L):
    R = nb * L
    acc = x * w[CONV_W - 1:CONV_W, :]
    if L == SUBLANES:
        tpos = lax.broadcasted_iota(jnp.int32, (R, 1), 0) % L
    else:
        tpos = lax.broadcasted_iota(jnp.int32, (SUBLANES, 1), 0)
    for j in range(1, CONV_W):
        xs = pltpu.roll(x, j, 0)
        shift = (j - (CONV_W - 1)) % (nb * SUBLANES)
        hr = hist if shift == 0 else pltpu.roll(hist, shift, 0)
        if L == SUBLANES:
            sh = jnp.where(tpos < j, hr, xs)
        else:
            parts = []
            for s in range(nb):
                head = jnp.where(tpos < j, hr[s * SUBLANES:(s + 1) * SUBLANES],
                                 xs[s * L:s * L + SUBLANES])
                parts += [head, xs[s * L + SUBLANES:(s + 1) * L]]
            sh = jnp.concatenate(parts, axis=0)
        acc = acc + sh * w[CONV_W - 1 - j:CONV_W - j, :]
    return acc


def _next_hist(x, nb, L):
    if L == SUBLANES:
        return pltpu.roll(x, nb * L - (SUBLANES - (CONV_W - 1)), 0)
    parts = [pltpu.roll(x[(s + 1) * L - SUBLANES:(s + 1) * L], CONV_W - 1, 0) for s in range(nb)]
    return parts[0] if nb == 1 else jnp.concatenate(parts, axis=0)


def _for_each_seq(nb, fn):
    if nb <= 4:
        for s in range(nb):
            fn(s)
    else:
        def body(s, carry):
            fn(s)
            return carry
        lax.fori_loop(0, nb, body, 0)


def _rows(s, L):
    start = s * L
    if not isinstance(start, int):
        start = pl.multiple_of(start, SUBLANES)
    return pl.ds(start, L)


def _rmsnorm_kernel(x_ref, w_ref, o_ref):
    x = x_ref[...]
    y = x * lax.rsqrt(jnp.mean(x * x, axis=-1, keepdims=True) + NORM_EPS)
    o_ref[...] = (y * w_ref[...]).astype(o_ref.dtype)


def _rmsnorm(x, w, out_dtype, tm=ROWS_NORM):
    n, d = x.shape
    return pl.pallas_call(
        _rmsnorm_kernel,
        grid=(n // tm,),
        in_specs=[pl.BlockSpec((tm, d), lambda i: (i, 0)), pl.BlockSpec((1, d), lambda i: (0, 0))],
        out_specs=pl.BlockSpec((tm, d), lambda i: (i, 0)),
        out_shape=jax.ShapeDtypeStruct((n, d), out_dtype),
        compiler_params=_params(("parallel",)),
        name="rmsnorm",
    )(x, w.reshape(1, d))


def _final_norm_kernel(n_first, x_ref, w_ref, a_ref, b_ref):
    x = x_ref[...]
    y = x * lax.rsqrt(jnp.mean(x * x, axis=-1, keepdims=True) + NORM_EPS) * w_ref[...]
    i = pl.program_id(0)

    @pl.when(i < n_first)
    def _():
        a_ref[...] = y

    @pl.when(i >= n_first)
    def _():
        b_ref[...] = y


def _final_norm(x, w, rows_a, rows_b, tm=ROWS_FINAL_NORM):
    d = x.shape[1]
    na, nb = rows_a // tm, rows_b // tm
    return pl.pallas_call(
        functools.partial(_final_norm_kernel, na),
        grid=(na + nb,),
        in_specs=[pl.BlockSpec((tm, d), lambda i: (i, 0)), pl.BlockSpec((1, d), lambda i: (0, 0))],
        out_specs=[pl.BlockSpec((tm, d), lambda i: (jnp.minimum(i, na - 1), 0)),
                   pl.BlockSpec((tm, d), lambda i: (jnp.maximum(i - na, 0), 0))],
        out_shape=[jax.ShapeDtypeStruct((rows_a, d), F32), jax.ShapeDtypeStruct((rows_b, d), F32)],
        compiler_params=_params(("arbitrary",)),
        name="final_norm",
    )(x, w.reshape(1, d))


def _layer_cols(layer, k, tn, col_block):
    return pl.BlockSpec((None, k, tn), lambda i, j: (layer, 0, col_block(j)))


def _layer_rows(layer, rows, k, row_block):
    return pl.BlockSpec((None, rows, k), lambda i, j: (layer, row_block(j), 0))


def _dot_wt(a, wt):
    return lax.dot_general(a, wt.astype(BF16), (((1,), (1,)), ((), ())), preferred_element_type=F32)


def _in_proj_a_kernel(a_ref, wt_ref, o_ref):
    o_ref[...] = _dot_wt(a_ref[...], wt_ref[...])


def _in_proj_s_kernel(a_ref, wt_ref, wnext_ref, o_ref):
    o_ref[...] = _dot_wt(a_ref[...], jnp.concatenate([wt_ref[SRC_SHIFT:, :], wnext_ref[...]], axis=0))


def _in_proj_gates_kernel(a_ref, wm_ref, g0_ref, g1_ref, g2_ref, g3_ref, o_ref):
    k = wm_ref.shape[1]
    wt = jnp.concatenate([wm_ref[...], g0_ref[...], g1_ref[...], g2_ref[...], g3_ref[...],
                          jnp.zeros((LANES - N_GATES, k), F32)], axis=0)
    o_ref[...] = _dot_wt(a_ref[...], wt)


def _in_proj(hn, w_in_t, layer, tm=ROWS_DENSE, tn=COLS_DENSE):
    m, k = hn.shape
    assert SRC_SHIFT == SUBLANES and w_in_t.shape[1:] == (N_IN, k)
    a_spec = pl.BlockSpec((tm, k), lambda i, j: (i, 0))
    out_spec = pl.BlockSpec((tm, tn), lambda i, j: (i, j))
    proj_a = pl.pallas_call(
        _in_proj_a_kernel,
        grid=(m // tm, N_A // tn),
        in_specs=[a_spec, _layer_rows(layer, tn, k, lambda j: j)],
        out_specs=out_spec,
        out_shape=jax.ShapeDtypeStruct((m, N_A), F32),
        compiler_params=_params(("parallel", "arbitrary")),
        name="in_proj_a",
    )(hn, w_in_t)
    first = (SRC_S - SRC_SHIFT) // tn
    proj_s = pl.pallas_call(
        _in_proj_s_kernel,
        grid=(m // tm, N_S // tn),
        in_specs=[a_spec, _layer_rows(layer, tn, k, lambda j: first + j),
                  _layer_rows(layer, SUBLANES, k, lambda j: (first + j + 1) * (tn // SUBLANES))],
        out_specs=out_spec,
        out_shape=jax.ShapeDtypeStruct((m, N_S), F32),
        compiler_params=_params(("parallel", "arbitrary")),
        name="in_proj_s",
    )(hn, w_in_t, w_in_t)
    gate_rows = [SRC_GATES_M] + [SRC_GATES_G + SRC_SHIFT + r * SUBLANES for r in range(4)]
    gates = pl.pallas_call(
        _in_proj_gates_kernel,
        grid=(m // tm, 1),
        in_specs=[a_spec] + [_layer_rows(layer, SUBLANES, k, lambda j, r=r: r // SUBLANES) for r in gate_rows],
        out_specs=pl.BlockSpec((tm, LANES), lambda i, j: (i, 0)),
        out_shape=jax.ShapeDtypeStruct((m, LANES), F32),
        compiler_params=_params(("parallel", "arbitrary")),
        name="in_proj_gates",
    )(hn, *([w_in_t] * len(gate_rows)))
    return proj_a, proj_s, gates


def _mm_swiglu_kernel(a_ref, wg_ref, wu_ref, o_ref):
    a = a_ref[...]
    g = jnp.dot(a, wg_ref[...].astype(BF16), preferred_element_type=F32)
    u = jnp.dot(a, wu_ref[...].astype(BF16), preferred_element_type=F32)
    o_ref[...] = (_silu(g) * u).astype(o_ref.dtype)


def _matmul_swiglu(a, wg, wu, layer, tm=ROWS_DENSE, tn=COLS_DENSE):
    m, k = a.shape
    n = wg.shape[2]
    return pl.pallas_call(
        _mm_swiglu_kernel,
        grid=(m // tm, n // tn),
        in_specs=[pl.BlockSpec((tm, k), lambda i, j: (i, 0)),
                  _layer_cols(layer, k, tn, lambda j: j), _layer_cols(layer, k, tn, lambda j: j)],
        out_specs=pl.BlockSpec((tm, tn), lambda i, j: (i, j)),
        out_shape=jax.ShapeDtypeStruct((m, n), BF16),
        compiler_params=_params(("parallel", "arbitrary")),
        name="ffn_gate_up",
    )(a, wg, wu)


def _mm_resid_kernel(widths, *refs):
    a_refs = refs[:len(widths)]
    w_ref, x_ref, o_ref = refs[len(widths):]
    acc = x_ref[...]
    k0 = 0
    for a_ref, kw in zip(a_refs, widths):
        acc = acc + jnp.dot(a_ref[...], w_ref[k0:k0 + kw, :].astype(BF16), preferred_element_type=F32)
        k0 += kw
    o_ref[...] = acc


def _matmul_resid(a_list, w, layer, x, tm, tn, name):
    m = x.shape[0]
    _, k, n = w.shape
    widths = tuple(a.shape[1] for a in a_list)
    return pl.pallas_call(
        functools.partial(_mm_resid_kernel, widths),
        grid=(m // tm, n // tn),
        in_specs=[pl.BlockSpec((tm, kw), lambda i, j: (i, 0)) for kw in widths]
        + [_layer_cols(layer, k, tn, lambda j: j), pl.BlockSpec((tm, tn), lambda i, j: (i, j))],
        out_specs=pl.BlockSpec((tm, tn), lambda i, j: (i, j)),
        out_shape=jax.ShapeDtypeStruct((m, n), F32),
        compiler_params=_params(("parallel", "arbitrary")),
        name=name,
    )(*a_list, w, x)


def _rglru_kernel(nb, L, _y_all_ref, rx_ref, rg_ref, hist0_ref, h0_ref, cw_ref, cb_ref, wa_ref, ba_ref, wx_ref,
                  bx_ref, lam_ref, y_ref, h_ref, hist_ref, a_s, u_s):
    @pl.when(pl.program_id(1) == 0)
    def _():
        h_ref[...] = h0_ref[...]
        hist_ref[...] = hist0_ref[...]

    x = rx_ref[...]
    xc = _conv4(x, hist_ref[...], cw_ref[...], nb, L) + cb_ref[...]
    hist_ref[...] = _next_hist(x, nb, L)

    xcb = xc.astype(BF16)
    ga = jnp.concatenate([jnp.dot(xcb[:, n * R_BDIM:(n + 1) * R_BDIM], wa_ref[n], preferred_element_type=F32)
                          for n in range(R_BLOCKS)], axis=1)
    gx = jnp.concatenate([jnp.dot(xcb[:, n * R_BDIM:(n + 1) * R_BDIM], wx_ref[n], preferred_element_type=F32)
                          for n in range(R_BLOCKS)], axis=1)
    r = jax.nn.sigmoid(ga + ba_ref[...])
    i = jax.nn.sigmoid(gx + bx_ref[...])
    log_a = -R_C * r * _softplus(-lam_ref[...])
    a_s[...] = jnp.exp(log_a)
    th = jnp.tanh(log_a)
    u_s[...] = jnp.sqrt(-2.0 * th / (1.0 - th)) * (i * xc)

    if nb == 1:
        def step(t, h):
            h = a_s[pl.ds(t, 1), :] * h + u_s[pl.ds(t, 1), :]
            u_s[pl.ds(t, 1), :] = h
            return h

        h_last = lax.fori_loop(0, L, step, h_ref[0:1, :], unroll=SUBLANES)
        h_ref[...] = jnp.broadcast_to(h_last, (SUBLANES, R_WIDTH))
        hs = u_s[...]
    else:
        a, u = a_s[...], u_s[...]
        tpos = lax.broadcasted_iota(jnp.int32, (nb * L, 1), 0) % L
        step = 1
        while step < L:
            valid = tpos >= step
            u = jnp.where(valid, a * pltpu.roll(u, step, 0) + u, u)
            a = jnp.where(valid, a * pltpu.roll(a, step, 0), a)
            step *= 2
        if L == SUBLANES:
            h0 = h_ref[...]
        else:
            h0 = jnp.concatenate([jnp.broadcast_to(h_ref[s * SUBLANES:s * SUBLANES + 1, :], (L, R_WIDTH))
                                  for s in range(nb)], axis=0)
        hs = a * h0 + u
        if L == SUBLANES:
            h_ref[...] = pltpu.roll(hs, nb * L - (L - 1), 0)
        else:
            h_ref[...] = jnp.concatenate([jnp.broadcast_to(hs[(s + 1) * L - 1:(s + 1) * L, :], (SUBLANES, R_WIDTH))
                                          for s in range(nb)], axis=0)
    y_ref[...] = (hs * jax.nn.gelu(rg_ref[...])).astype(y_ref.dtype)


def _rglru_group(pm, y_all, row0, nseq, nb, L, nch, hist0, h0, prm):
    R = nb * L
    base = row0 // R
    nsb = nseq // nb
    rb = lambda sb, c: base + sb * nch + c
    full = lambda shape: pl.BlockSpec(shape, lambda sb, c: (0,) * len(shape))
    st = pl.BlockSpec((nb * SUBLANES, R_WIDTH), lambda sb, c: (sb, 0))
    return pl.pallas_call(
        functools.partial(_rglru_kernel, nb, L),
        grid=(nsb, nch),
        in_specs=[pl.BlockSpec(memory_space=pl.ANY),
                  pl.BlockSpec((R, R_WIDTH), lambda sb, c: (rb(sb, c), COL_RX // R_WIDTH)),
                  pl.BlockSpec((R, R_WIDTH), lambda sb, c: (rb(sb, c), COL_RG // R_WIDTH)),
                  st, st,
                  full((CONV_W, R_WIDTH)), full((1, R_WIDTH)),
                  full((R_BLOCKS, R_BDIM, R_BDIM)), full((1, R_WIDTH)),
                  full((R_BLOCKS, R_BDIM, R_BDIM)), full((1, R_WIDTH)), full((1, R_WIDTH))],
        out_specs=[pl.BlockSpec((R, R_WIDTH), lambda sb, c: (rb(sb, c), 0)), st, st],
        out_shape=[jax.ShapeDtypeStruct(y_all.shape, y_all.dtype),
                   jax.ShapeDtypeStruct((nseq * SUBLANES, R_WIDTH), F32),
                   jax.ShapeDtypeStruct((nseq * SUBLANES, R_WIDTH), F32)],
        scratch_shapes=[pltpu.VMEM((R, R_WIDTH), F32), pltpu.VMEM((R, R_WIDTH), F32)],
        compiler_params=_params(("parallel", "arbitrary")),
        input_output_aliases={0: 0},
        name="rglru",
    )(y_all, pm, pm, hist0, h0, *prm)


def _mlstm_kernel(nb, L, n_passthrough, *refs):
    (q_ref, k_ref, v_ref, o_ref, g_ref, bias_ref, nw_ref, c0_ref, n0_ref, m0_ref,
     y_ref, c_ref, n_ref, m_ref, num_s, col_s) = refs[n_passthrough:]
    R = nb * L

    @pl.when(pl.program_id(1) == 0)
    def _():
        c_ref[...] = c0_ref[...]
        n_ref[...] = n0_ref[...]
        m_ref[...] = m0_ref[...]

    z = g_ref[...] + bias_ref[...]
    lf = pltpu.roll(-_softplus(-z), LANES - GATE_MF, 1)
    causal, _ = _seq_masks(nb, L)
    fcum = _cumsum_rows(lf, nb, L)
    bcol = z - fcum
    brow = bcol.T
    if L == SUBLANES:
        m_rows = m_ref[...]
    else:
        m_rows = jnp.concatenate(
            [jnp.broadcast_to(m_ref[s * SUBLANES:s * SUBLANES + 1, :], (L, LANES)) for s in range(nb)], axis=0)

    for h in range(M_HEADS):
        fc = fcum[:, h:h + 1]
        d = fc + brow[h:h + 1, :]
        rowmax = jnp.max(jnp.where(causal, d, -jnp.inf), axis=-1, keepdims=True)
        inter = m_rows[:, h:h + 1] + fc
        mt = jnp.maximum(rowmax, inter)
        p = jnp.where(causal, jnp.exp(jnp.where(causal, d - mt, 0.0)), 0.0)
        qh = q_ref[:, h * M_DK:(h + 1) * M_DK] * (M_DK ** -0.5)
        s = _dot_nt(qh, k_ref[:, h * M_DK:(h + 1) * M_DK]) * p
        num_s[:, h * M_DV:(h + 1) * M_DV] = _dot(s, v_ref[:, h * M_DV:(h + 1) * M_DV])
        for idx, val in enumerate((mt, jnp.exp(inter - mt), jnp.sum(s, axis=-1, keepdims=True),
                                   bcol[:, h:h + 1], fc)):
            col_s[idx, h] = jnp.broadcast_to(val, (R, LANES))

    def per_seq(s):
        rows = _rows(s, L)
        mrow = pl.ds(s * SUBLANES, 1)
        for h in range(M_HEADS):
            dk = slice(h * M_DK, (h + 1) * M_DK)
            dv = slice(h * M_DV, (h + 1) * M_DV)
            mt = col_s[0, h, rows, 0:1]
            w_inter = col_s[1, h, rows, 0:1]
            den_in = col_s[2, h, rows, 0:1]
            bc = col_s[3, h, rows, 0:1]
            f_last = col_s[4, h, rows, 0:1][L - 1:L]
            qh = q_ref[rows, dk] * (M_DK ** -0.5)
            kh = k_ref[rows, dk]
            vh = v_ref[rows, dv]
            c_old = c_ref[s, h]
            n_old = n_ref[s, h:h + 1, :]
            m_old = m_ref[mrow, h:h + 1]
            num = w_inter * _dot(qh, c_old) + num_s[rows, dv]
            den = w_inter * jnp.sum(qh * n_old, axis=-1, keepdims=True) + den_in
            hh = num / jnp.maximum(jnp.abs(den), jnp.exp(-mt))
            hn = hh * lax.rsqrt(jnp.mean(hh * hh, axis=-1, keepdims=True) + NORM_EPS) * nw_ref[:, dv]
            y_ref[rows, dv] = (hn * jax.nn.sigmoid(o_ref[rows, dv])).astype(y_ref.dtype)
            m_new = mt[L - 1:L]
            kw = kh * jnp.exp(bc + (f_last - m_new))
            scale = jnp.exp(m_old + f_last - m_new)
            c_ref[s, h] = scale * c_old + _dot_tn(kw, vh)
            n_ref[s, h:h + 1, :] = scale * n_old + jnp.sum(kw, axis=0, keepdims=True)
            m_ref[pl.ds(s * SUBLANES, SUBLANES), h:h + 1] = jnp.broadcast_to(m_new, (SUBLANES, 1))

    _for_each_seq(nb, per_seq)


def _stacked_state(layer, block):
    zeros = (0,) * (len(block) - 1)
    return pl.BlockSpec((None,) + block, lambda sb, c: (layer, sb) + zeros)


def _alias_inputs(y_all, state_out):
    prev, _, _ = state_out
    operands = [y_all] + ([prev] if prev is not None else [])
    return operands, {i: i for i in range(len(operands))}


def _mlstm_group(pm, pg, y_all, row0, nseq, nb, L, nch, c_in, n0, m0, bias_row, norm_row, c_out):
    R = nb * L
    base = row0 // R
    nsb = nseq // nb
    rb = lambda sb, c: base + sb * nch + c
    full = lambda shape: pl.BlockSpec(shape, lambda sb, c: (0,) * len(shape))
    cblock = (nb, M_HEADS, M_DK, M_DV)
    nst = pl.BlockSpec((nb, M_HEADS, M_DK), lambda sb, c: (sb, 0, 0))
    mst = pl.BlockSpec((nb * SUBLANES, LANES), lambda sb, c: (sb, 0))
    passthrough, aliases = _alias_inputs(y_all, c_out)
    return pl.pallas_call(
        functools.partial(_mlstm_kernel, nb, L, len(passthrough)),
        grid=(nsb, nch),
        in_specs=[pl.BlockSpec(memory_space=pl.ANY)] * len(passthrough) + [
                  pl.BlockSpec((R, M_HEADS * M_DK), lambda sb, c: (rb(sb, c), COL_MQ // (M_HEADS * M_DK))),
                  pl.BlockSpec((R, M_HEADS * M_DK), lambda sb, c: (rb(sb, c), COL_MK // (M_HEADS * M_DK))),
                  pl.BlockSpec((R, M_WIDTH), lambda sb, c: (rb(sb, c), COL_MV // M_WIDTH)),
                  pl.BlockSpec((R, M_WIDTH), lambda sb, c: (rb(sb, c), COL_MO // M_WIDTH)),
                  pl.BlockSpec((R, LANES), lambda sb, c: (rb(sb, c), 0)),
                  full((1, LANES)), full((1, M_WIDTH)), _stacked_state(c_in[1], cblock), nst, mst],
        out_specs=[pl.BlockSpec((R, M_WIDTH), lambda sb, c: (rb(sb, c), 0)),
                   _stacked_state(c_out[1], cblock), nst, mst],
        out_shape=[jax.ShapeDtypeStruct(y_all.shape, y_all.dtype),
                   jax.ShapeDtypeStruct((c_out[2], nseq, M_HEADS, M_DK, M_DV), F32),
                   jax.ShapeDtypeStruct((nseq, M_HEADS, M_DK), F32),
                   jax.ShapeDtypeStruct((nseq * SUBLANES, LANES), F32)],
        scratch_shapes=[pltpu.VMEM((R, M_WIDTH), F32), pltpu.VMEM((5, M_HEADS, R, LANES), F32)],
        compiler_params=_params(("parallel", "arbitrary")),
        input_output_aliases=aliases,
        name="mlstm",
    )(*passthrough, pm, pm, pm, pm, pg, bias_row, norm_row, c_in[0], n0, m0)


GDN_HEAD_GROUP = 16
GDN_SOLVE_ROWS = 64


def _gdn_kernel(nb, L, n_passthrough, *refs):
    (q_ref, k_ref, v_ref, z_ref, g_ref, cw_ref, hist0_ref, alog_ref, dtb_ref, nw_ref, s0_ref,
     y_ref, s_ref, hist_ref, k_s, kq_s, vb_s, t_s, at_s, ws_s, vn_s, g_s) = refs[n_passthrough:]
    R = nb * L

    @pl.when(pl.program_id(1) == 0)
    def _():
        s_ref[...] = s0_ref[...]
        hist_ref[...] = hist0_ref[...]

    zt = g_ref[...]
    beta = pltpu.roll(jax.nn.sigmoid(zt), LANES - GATE_GB, 1)
    gdec = pltpu.roll(-jnp.exp(alog_ref[...]) * _softplus(zt + dtb_ref[...]), LANES - GATE_GA, 1)
    causal, strict = _seq_masks(nb, L)
    gcum = _cumsum_rows(gdec, nb, L)
    g_s[...] = gcum
    grow = gcum.T

    eye = (lax.broadcasted_iota(jnp.int32, (R, R), 0) == lax.broadcasted_iota(jnp.int32, (R, R), 1)).astype(F32)
    block = min(L, GDN_SOLVE_ROWS)
    two_blocks = L > block
    assert L in (block, 2 * block)
    levels = int(math.log2(block)) - 1
    if two_blocks:
        diag_blocks = (lax.broadcasted_iota(jnp.int32, (R, R), 0) // block
                       == lax.broadcasted_iota(jnp.int32, (R, R), 1) // block)
    for h0 in range(0, G_HEADS, GDN_HEAD_GROUP):
        heads = range(h0, h0 + GDN_HEAD_GROUP)
        lhs, ks, decays = [], [], []
        for h in heads:
            sl = slice(h * G_DK, (h + 1) * G_DK)
            parts = []
            for idx, ref in enumerate((q_ref, k_ref, v_ref)):
                cs = slice(idx * G_WIDTH + h * G_DK, idx * G_WIDTH + (h + 1) * G_DK)
                parts.append(_silu(_conv4(ref[:, sl], hist_ref[:, cs], cw_ref[:, cs], nb, L)))
            qh, kh, vh = parts
            q = qh * lax.rsqrt(jnp.sum(qh * qh, axis=-1, keepdims=True) + NORM_EPS) * (G_DK ** -0.5)
            k = kh * lax.rsqrt(jnp.sum(kh * kh, axis=-1, keepdims=True) + NORM_EPS)
            bc = beta[:, h:h + 1]
            gc = gcum[:, h:h + 1]
            decays.append(jnp.where(causal, jnp.exp(jnp.where(causal, gc - grow[h:h + 1, :], 0.0)), 0.0))
            kb = k * bc
            eg = jnp.exp(gc)
            k_s[h] = k
            kq_s[h, 0] = kb * eg
            kq_s[h, 1] = q * eg
            vb_s[h] = vh * bc
            lhs.append(jnp.concatenate([kb, q], axis=0))
            ks.append(k)
        aas = [_dot_nt(l, k) for l, k in zip(lhs, ks)]
        avs = [jnp.where(strict, aa[:R] * d, 0.0) for aa, d in zip(aas, decays)]
        for h, aa, d in zip(heads, aas, decays):
            at_s[h] = aa[R:] * d
        if two_blocks:
            offs = [jnp.where(diag_blocks, 0.0, a) for a in avs]
            avs = [jnp.where(diag_blocks, a, 0.0) for a in avs]
        ps = [eye - a for a in avs]
        bs = [_dot(a, a) for a in avs]
        for lvl in range(levels):
            if lvl + 1 < levels:
                xs = [_dot(b, jnp.concatenate([b, p], axis=1)) for b, p in zip(bs, ps)]
                bs = [x[:, :R] for x in xs]
                ps = [p + x[:, R:] for p, x in zip(ps, xs)]
            else:
                ps = [p + _dot(b, p) for b, p in zip(bs, ps)]
        if two_blocks:
            ots = [_dot(o, p) for o, p in zip(offs, ps)]
            ps = [p - _dot(p, ot) for p, ot in zip(ps, ots)]
        for h, p in zip(heads, ps):
            t_s[h] = p

    def read_state(s):
        rows = _rows(s, L)
        for h in range(G_HEADS):
            x = _dot(jnp.concatenate([kq_s[h, 0, rows, :], kq_s[h, 1, rows, :]], axis=0), s_ref[s, h])
            ws_s[h, 0, rows, :] = x[:L]
            ws_s[h, 1, rows, :] = x[L:]

    _for_each_seq(nb, read_state)

    for h in range(G_HEADS):
        sl = slice(h * G_DK, (h + 1) * G_DK)
        v_new = _dot(t_s[h], vb_s[h] - ws_s[h, 0])
        vn_s[h] = v_new
        o = ws_s[h, 1] + _dot(at_s[h], v_new)
        on = o * lax.rsqrt(jnp.mean(o * o, axis=-1, keepdims=True) + NORM_EPS) * nw_ref[...]
        y_ref[:, sl] = (on * _silu(z_ref[:, sl])).astype(y_ref.dtype)

    def write_state(s):
        rows = _rows(s, L)
        for h in range(G_HEADS):
            gseq = g_s[rows, h:h + 1]
            g_end = gseq[L - 1:L]
            kdec = k_s[h, rows, :] * jnp.exp(g_end - gseq)
            s_ref[s, h] = s_ref[s, h] * jnp.exp(g_end) + _dot_tn(kdec, vn_s[h, rows, :])

    _for_each_seq(nb, write_state)

    for idx, ref in enumerate((q_ref, k_ref, v_ref)):
        hist_ref[:, idx * G_WIDTH:(idx + 1) * G_WIDTH] = _next_hist(ref[...], nb, L)


def _gdn_group(pm, pg, y_all, row0, nseq, nb, L, nch, s_in, hist0, prm, s_out):
    R = nb * L
    base = row0 // R
    nsb = nseq // nb
    rb = lambda sb, c: base + sb * nch + c
    full = lambda shape: pl.BlockSpec(shape, lambda sb, c: (0,) * len(shape))
    sblock = (nb, G_HEADS, G_DK, G_DV)
    hst = pl.BlockSpec((nb * SUBLANES, G_QKV), lambda sb, c: (sb, 0))
    col = lambda off: pl.BlockSpec((R, G_WIDTH), lambda sb, c: (rb(sb, c), off // G_WIDTH))
    passthrough, aliases = _alias_inputs(y_all, s_out)
    return pl.pallas_call(
        functools.partial(_gdn_kernel, nb, L, len(passthrough)),
        grid=(nsb, nch),
        in_specs=[pl.BlockSpec(memory_space=pl.ANY)] * len(passthrough) + [
                  col(COL_GQ), col(COL_GK), col(COL_GV), col(COL_GZ),
                  pl.BlockSpec((R, LANES), lambda sb, c: (rb(sb, c), 0)),
                  full((CONV_W, G_QKV)), hst, full((1, LANES)), full((1, LANES)), full((1, G_DV)),
                  _stacked_state(s_in[1], sblock)],
        out_specs=[pl.BlockSpec((R, G_WIDTH), lambda sb, c: (rb(sb, c), 0)),
                   _stacked_state(s_out[1], sblock), hst],
        out_shape=[jax.ShapeDtypeStruct(y_all.shape, y_all.dtype),
                   jax.ShapeDtypeStruct((s_out[2], nseq, G_HEADS, G_DK, G_DV), F32),
                   jax.ShapeDtypeStruct((nseq * SUBLANES, G_QKV), F32)],
        scratch_shapes=[pltpu.VMEM((G_HEADS, R, G_DK), F32), pltpu.VMEM((G_HEADS, 2, R, G_DK), F32),
                        pltpu.VMEM((G_HEADS, R, G_DV), F32), pltpu.VMEM((G_HEADS, R, R), F32),
                        pltpu.VMEM((G_HEADS, R, R), F32), pltpu.VMEM((G_HEADS, 2, R, G_DV), F32),
                        pltpu.VMEM((G_HEADS, R, G_DV), F32), pltpu.VMEM((R, LANES), F32)],
        compiler_params=_params(("parallel", "arbitrary")),
        input_output_aliases=aliases,
        name="gdn",
    )(*passthrough, pm, pm, pm, pm, pg, prm[0], hist0, prm[1], prm[2], prm[3], s_in[0])


def _lane_row(vec, offset):
    return jnp.zeros((1, LANES), F32).at[0, offset:offset + vec.shape[0]].set(vec.astype(F32))


def _pad_hist(buf):
    n, w, c = buf.shape
    return jnp.pad(buf.astype(F32), ((0, 0), (0, SUBLANES - w), (0, 0))).reshape(n * SUBLANES, c)


def _unpad_hist(hist, n):
    return hist.reshape(n, SUBLANES, hist.shape[-1])[:, :CONV_W - 1]


def _mixers(proj, mix, group, states, big_out, lw):
    pa, ps, pg = proj
    row0, nseq, nb_m, nb_r, nb_g, L_m, L_r, L_g, T = group
    c_in, n0, m0, h0, rhist, s_in, ghist = states
    ym, c, n, m = _mlstm_group(pa, pg, mix[0], row0, nseq, nb_m, L_m, T // L_m, c_in, n0, m0,
                               lw["m_bias"], lw["m_norm"], big_out[0])
    yr, h, rh = _rglru_group(ps, mix[1], row0, nseq, nb_r, L_r, T // L_r, rhist, h0, lw["r_prm"])
    yg, s, gh = _gdn_group(ps, pg, mix[2], row0, nseq, nb_g, L_g, T // L_g, s_in, ghist, lw["g_prm"], big_out[1])
    return (ym, yr, yg), (c, n, m, h, rh, s, gh)


def kernel(x_prompt, x_sample, state_mlstm_C, state_mlstm_n, state_mlstm_m, state_rglru_h, state_rglru_conv, state_gdn_S, state_gdn_conv, meta_tokens, norm_mix, w_in, m_bias_i, m_bias_f, m_norm, r_conv_w, r_conv_b, r_gate_a_w, r_gate_a_b, r_gate_x_w, r_gate_x_b, r_lambda, g_conv_w, g_A_log, g_dt_bias, g_norm, w_out, norm_ffn, w_gate, w_up, w_down, norm_final):
    batch, seq, d = x_prompt.shape
    dec_batch, dec_seq, _ = x_sample.shape
    depth = w_in.shape[0]
    n_prompt = batch * seq
    n_sample = dec_batch * dec_seq
    n_rows = n_prompt + n_sample + batch * N_META

    meta = jnp.broadcast_to(meta_tokens.astype(F32)[None], (batch, N_META, d))
    x = jnp.concatenate([x_prompt.reshape(n_prompt, d), x_sample.reshape(n_sample, d),
                         meta.reshape(batch * N_META, d)], axis=0)

    g_meta = (n_prompt + n_sample, batch, batch, batch, batch, N_META, N_META, N_META, N_META)
    g_prompt = (0, batch, 1, 1, 1, 512, 256, 128, seq)
    g_sample = (n_prompt, dec_batch, 16, 32, 8, dec_seq, dec_seq, dec_seq, dec_seq)

    zeros = lambda *shape: jnp.zeros(shape, F32)
    meta_states = ((zeros(1, batch, M_HEADS, M_DK, M_DV), 0), zeros(batch, M_HEADS, M_DK),
                   zeros(batch * SUBLANES, LANES), zeros(batch * SUBLANES, R_WIDTH), zeros(batch * SUBLANES, R_WIDTH),
                   (zeros(1, batch, G_HEADS, G_DK, G_DV), 0), zeros(batch * SUBLANES, G_QKV))
    one_layer = ((None, 0, 1), (None, 0, 1))
    w_down_bf16 = w_down.astype(BF16)
    w_in_t = jnp.swapaxes(w_in, 1, 2)

    prompt_new, sample_new = [], []
    sample_c = sample_s = None
    for l in range(depth):
        lw = dict(
            m_bias=_lane_row(m_bias_i[l], GATE_MI) + _lane_row(m_bias_f[l], GATE_MF),
            m_norm=m_norm[l].reshape(1, M_WIDTH),
            r_prm=(r_conv_w[l], r_conv_b[l].reshape(1, R_WIDTH), r_gate_a_w[l].astype(BF16),
                   r_gate_a_b[l].reshape(1, R_WIDTH), r_gate_x_w[l].astype(BF16),
                   r_gate_x_b[l].reshape(1, R_WIDTH), r_lambda[l].reshape(1, R_WIDTH)),
            g_prm=(g_conv_w[l], _lane_row(g_A_log[l], GATE_GA), _lane_row(g_dt_bias[l], GATE_GA),
                   g_norm[l].reshape(1, G_DV)),
        )
        sample_states = ((state_mlstm_C, l), state_mlstm_n[l],
                         jnp.pad(jnp.repeat(state_mlstm_m[l], SUBLANES, axis=0), ((0, 0), (0, LANES - M_HEADS))),
                         jnp.repeat(state_rglru_h[l], SUBLANES, axis=0), _pad_hist(state_rglru_conv[l]),
                         (state_gdn_S, l), _pad_hist(state_gdn_conv[l]))

        hn = _rmsnorm(x, norm_mix[l], BF16)
        proj = _in_proj(hn, w_in_t, l)

        def as_input(st):
            return ((st[0], 0),) + st[1:5] + ((st[5], 0), st[6])

        mix = tuple(jnp.zeros((n_rows, width), BF16) for width in (M_WIDTH, R_WIDTH, G_WIDTH))
        mix, st_meta = _mixers(proj, mix, g_meta, meta_states, one_layer, lw)
        mix, st_prompt = _mixers(proj, mix, g_prompt, as_input(st_meta), one_layer, lw)
        mix, st_sample = _mixers(proj, mix, g_sample, sample_states,
                                 ((sample_c, l, depth), (sample_s, l, depth)), lw)
        sample_c, sample_s = st_sample[0], st_sample[5]
        prompt_new.append(st_prompt)
        sample_new.append(st_sample)

        x = _matmul_resid(mix, w_out, l, x, ROWS_DENSE, COLS_DENSE, "out_proj")
        hf = _rmsnorm(x, norm_ffn[l], BF16)
        hmid = _matmul_swiglu(hf, w_gate, w_up, l)
        x = _matmul_resid([hmid], w_down_bf16, l, x, ROWS_DOWN, COLS_DOWN, "ffn_down")

    y_prompt_out, y_sample_out = _final_norm(x, norm_final, n_prompt, n_sample)
    y_prompt_out = y_prompt_out.reshape(batch, seq, d)
    y_sample_out = y_sample_out.reshape(dec_batch, dec_seq, d)

    def small_states(per_layer, n):
        nn, m, h, rh, gh = (jnp.stack([st[i] for st in per_layer]) for i in (1, 2, 3, 4, 6))
        return (nn, m[:, ::SUBLANES, :M_HEADS], h[:, ::SUBLANES],
                jax.vmap(lambda a: _unpad_hist(a, n))(rh), jax.vmap(lambda a: _unpad_hist(a, n))(gh))

    pn, pm_, ph, prh, pgh = small_states(prompt_new, batch)
    sn, sm, sh, srh, sgh = small_states(sample_new, dec_batch)
    prompt_c = jnp.concatenate([st[0] for st in prompt_new], axis=0)
    prompt_s = jnp.concatenate([st[5] for st in prompt_new], axis=0)
    return (y_prompt_out, y_sample_out, prompt_c, pn, pm_, ph, prh, prompt_s, pgh,
            sample_c, sn, sm, sh, srh, sample_s, sgh)
```

```python
import functools
import math

import jax
import jax.numpy as jnp
from jax import lax
from jax.experimental import pallas as pl
from jax.experimental.pallas import tpu as pltpu

F32 = jnp.float32
BF16 = jnp.bfloat16

N_META = 16
CONV_W = 4
NORM_EPS = 1e-6
M_HEADS = 4
M_DV = 256
M_DK = 128
M_WIDTH = M_HEADS * M_DV
R_WIDTH = 1024
R_BLOCKS = 8
R_BDIM = R_WIDTH // R_BLOCKS
R_C = 8.0
G_DK = 128
G_DV = 128
G_HEADS = 16
G_WIDTH = G_HEADS * G_DV
G_QKV = 3 * G_WIDTH

LANES = 128
SUBLANES = 8
VMEM_LIMIT = 56 * 2 ** 20

SRC_GATES_M = 3072
SRC_S = 3080
SRC_SHIFT = SRC_S % LANES
N_A = 3072
N_S = 10240
SRC_GATES_G = SRC_S + N_S - SRC_SHIFT
N_IN = SRC_S + N_S + 32
COL_MQ, COL_MK, COL_MV, COL_MO = 0, 512, 1024, 2048
COL_RX, COL_RG, COL_GQ, COL_GK, COL_GV, COL_GZ = 0, 1024, 2048, 4096, 6144, 8192
GATE_MI, GATE_MF, GATE_GB, GATE_GA = 0, 4, 8, 24
N_GATES = 40

ROWS_DENSE = 1856
ROWS_DOWN = 928
COLS_DENSE = 256
COLS_DOWN = 256
ROWS_NORM = 464
ROWS_FINAL_NORM = 512

MLSTM_SEQ_UNROLL = 2
GDN_SEQ_UNROLL = 8


def _params(semantics):
    return pltpu.CompilerParams(dimension_semantics=semantics, vmem_limit_bytes=VMEM_LIMIT)


def _dot(a, b):
    return jnp.dot(a.astype(BF16), b.astype(BF16), preferred_element_type=F32)


def _dot_nt(a, b):
    return lax.dot_general(a.astype(BF16), b.astype(BF16), (((1,), (1,)), ((), ())),
                           preferred_element_type=F32)


def _dot_tn(a, b):
    return lax.dot_general(a.astype(BF16), b.astype(BF16), (((0,), (0,)), ((), ())),
                           preferred_element_type=F32)


def _softplus(x):
    return jnp.maximum(x, 0.0) + jnp.log1p(jnp.exp(-jnp.abs(x)))


def _silu(x):
    return x * jax.nn.sigmoid(x)


def _seq_masks(nb, L):
    R = nb * L
    row = lax.broadcasted_iota(jnp.int32, (R, R), 0)
    col = lax.broadcasted_iota(jnp.int32, (R, R), 1)
    causal = row >= col
    strict = row > col
    if nb > 1:
        same = (row // L) == (col // L)
        causal = jnp.logical_and(causal, same)
        strict = jnp.logical_and(strict, same)
    return causal, strict


def _cumsum_rows(x, nb, L):
    tpos = lax.broadcasted_iota(jnp.int32, (nb * L, 1), 0) % L
    step = 1
    while step < L:
        x = x + jnp.where(tpos >= step, pltpu.roll(x, step, 0), 0.0)
        step *= 2
    return x


def _conv4(x, hist, w, nb, L):
    R = nb * L
    acc = x * w[CONV_W - 1:CONV_W, :]
    if L == SUBLANES:
        tpos = lax.broadcasted_iota(jnp.int32, (R, 1), 0) % L
    else:
        tpos = lax.broadcasted_iota(jnp.int32, (SUBLANES, 1), 0)
    for j in range(1, CONV_W):
        xs = pltpu.roll(x, j, 0)
        shift = (j - (CONV_W - 1)) % (nb * SUBLANES)
        hr = hist if shift == 0 else pltpu.roll(hist, shift, 0)
        if L == SUBLANES:
            sh = jnp.where(tpos < j, hr, xs)
        else:
            parts = []
            for s in range(nb):
                head = jnp.where(tpos < j, hr[s * SUBLANES:(s + 1) * SUBLANES],
                                 xs[s * L:s * L + SUBLANES])
                parts += [head, xs[s * L + SUBLANES:(s + 1) * L]]
            sh = jnp.concatenate(parts, axis=0)
        acc = acc + sh * w[CONV_W - 1 - j:CONV_W - j, :]
    return acc


def _next_hist(x, nb, L):
    if L == SUBLANES:
        return pltpu.roll(x, nb * L - (SUBLANES - (CONV_W - 1)), 0)
    parts = [pltpu.roll(x[(s + 1) * L - SUBLANES:(s + 1) * L], CONV_W - 1, 0) for s in range(nb)]
    return parts[0] if nb == 1 else jnp.concatenate(parts, axis=0)


def _for_each_seq(nb, fn, unroll):
    if nb <= 4:
        for s in range(nb):
            fn(s)
    else:
        def body(s, carry):
            fn(s)
            return carry
        lax.fori_loop(0, nb, body, 0, unroll=unroll)


def _rows(s, L):
    start = s * L
    if not isinstance(start, int):
        start = pl.multiple_of(start, SUBLANES)
    return pl.ds(start, L)


def _rmsnorm_kernel(x_ref, w_ref, o_ref):
    x = x_ref[...]
    y = x * lax.rsqrt(jnp.mean(x * x, axis=-1, keepdims=True) + NORM_EPS)
    o_ref[...] = (y * w_ref[...]).astype(o_ref.dtype)


def _rmsnorm(x, w, out_dtype, tm=ROWS_NORM):
    n, d = x.shape
    return pl.pallas_call(
        _rmsnorm_kernel,
        grid=(n // tm,),
        in_specs=[pl.BlockSpec((tm, d), lambda i: (i, 0)), pl.BlockSpec((1, d), lambda i: (0, 0))],
        out_specs=pl.BlockSpec((tm, d), lambda i: (i, 0)),
        out_shape=jax.ShapeDtypeStruct((n, d), out_dtype),
        compiler_params=_params(("parallel",)),
        name="rmsnorm",
    )(x, w.reshape(1, d))


def _final_norm_kernel(n_first, x_ref, w_ref, a_ref, b_ref):
    x = x_ref[...]
    y = x * lax.rsqrt(jnp.mean(x * x, axis=-1, keepdims=True) + NORM_EPS) * w_ref[...]
    i = pl.program_id(0)

    @pl.when(i < n_first)
    def _():
        a_ref[...] = y

    @pl.when(i >= n_first)
    def _():
        b_ref[...] = y


def _final_norm(x, w, rows_a, rows_b, tm=ROWS_FINAL_NORM):
    d = x.shape[1]
    na, nb = rows_a // tm, rows_b // tm
    return pl.pallas_call(
        functools.partial(_final_norm_kernel, na),
        grid=(na + nb,),
        in_specs=[pl.BlockSpec((tm, d), lambda i: (i, 0)), pl.BlockSpec((1, d), lambda i: (0, 0))],
        out_specs=[pl.BlockSpec((tm, d), lambda i: (jnp.minimum(i, na - 1), 0)),
                   pl.BlockSpec((tm, d), lambda i: (jnp.maximum(i - na, 0), 0))],
        out_shape=[jax.ShapeDtypeStruct((rows_a, d), F32), jax.ShapeDtypeStruct((rows_b, d), F32)],
        compiler_params=_params(("arbitrary",)),
        name="final_norm",
    )(x, w.reshape(1, d))


def _layer_cols(layer, k, tn, col_block):
    return pl.BlockSpec((None, k, tn), lambda i, j: (layer, 0, col_block(j)))


def _layer_rows(layer, rows, k, row_block):
    return pl.BlockSpec((None, rows, k), lambda i, j: (layer, row_block(j), 0))


def _dot_wt(a, wt):
    return lax.dot_general(a, wt.astype(BF16), (((1,), (1,)), ((), ())), preferred_element_type=F32)


def _in_proj_a_kernel(a_ref, wt_ref, o_ref):
    o_ref[...] = _dot_wt(a_ref[...], wt_ref[...])


def _in_proj_s_kernel(a_ref, wt_ref, wnext_ref, o_ref):
    o_ref[...] = _dot_wt(a_ref[...], jnp.concatenate([wt_ref[SRC_SHIFT:, :], wnext_ref[...]], axis=0))


def _in_proj_gates_kernel(a_ref, wm_ref, g0_ref, g1_ref, g2_ref, g3_ref, o_ref):
    k = wm_ref.shape[1]
    wt = jnp.concatenate([wm_ref[...], g0_ref[...], g1_ref[...], g2_ref[...], g3_ref[...],
                          jnp.zeros((LANES - N_GATES, k), F32)], axis=0)
    o_ref[...] = _dot_wt(a_ref[...], wt)


def _in_proj(hn, w_in_t, layer, tm=ROWS_DENSE, tn=COLS_DENSE):
    m, k = hn.shape
    assert SRC_SHIFT == SUBLANES and w_in_t.shape[1:] == (N_IN, k)
    a_spec = pl.BlockSpec((tm, k), lambda i, j: (i, 0))
    out_spec = pl.BlockSpec((tm, tn), lambda i, j: (i, j))
    proj_a = pl.pallas_call(
        _in_proj_a_kernel,
        grid=(m // tm, N_A // tn),
        in_specs=[a_spec, _layer_rows(layer, tn, k, lambda j: j)],
        out_specs=out_spec,
        out_shape=jax.ShapeDtypeStruct((m, N_A), F32),
        compiler_params=_params(("parallel", "arbitrary")),
        name="in_proj_a",
    )(hn, w_in_t)
    first = (SRC_S - SRC_SHIFT) // tn
    proj_s = pl.pallas_call(
        _in_proj_s_kernel,
        grid=(m // tm, N_S // tn),
        in_specs=[a_spec, _layer_rows(layer, tn, k, lambda j: first + j),
                  _layer_rows(layer, SUBLANES, k, lambda j: (first + j + 1) * (tn // SUBLANES))],
        out_specs=out_spec,
        out_shape=jax.ShapeDtypeStruct((m, N_S), F32),
        compiler_params=_params(("parallel", "arbitrary")),
        name="in_proj_s",
    )(hn, w_in_t, w_in_t)
    gate_rows = [SRC_GATES_M] + [SRC_GATES_G + SRC_SHIFT + r * SUBLANES for r in range(4)]
    gates = pl.pallas_call(
        _in_proj_gates_kernel,
        grid=(m // tm, 1),
        in_specs=[a_spec] + [_layer_rows(layer, SUBLANES, k, lambda j, r=r: r // SUBLANES) for r in gate_rows],
        out_specs=pl.BlockSpec((tm, LANES), lambda i, j: (i, 0)),
        out_shape=jax.ShapeDtypeStruct((m, LANES), F32),
        compiler_params=_params(("parallel", "arbitrary")),
        name="in_proj_gates",
    )(hn, *([w_in_t] * len(gate_rows)))
    return proj_a, proj_s, gates


def _mm_swiglu_kernel(a_ref, wg_ref, wu_ref, o_ref):
    a = a_ref[...]
    g = jnp.dot(a, wg_ref[...].astype(BF16), preferred_element_type=F32)
    u = jnp.dot(a, wu_ref[...].astype(BF16), preferred_element_type=F32)
    o_ref[...] = (_silu(g) * u).astype(o_ref.dtype)


def _matmul_swiglu(a, wg, wu, layer, tm=ROWS_DENSE, tn=COLS_DENSE):
    m, k = a.shape
    n = wg.shape[2]
    return pl.pallas_call(
        _mm_swiglu_kernel,
        grid=(m // tm, n // tn),
        in_specs=[pl.BlockSpec((tm, k), lambda i, j: (i, 0)),
                  _layer_cols(layer, k, tn, lambda j: j), _layer_cols(layer, k, tn, lambda j: j)],
        out_specs=pl.BlockSpec((tm, tn), lambda i, j: (i, j)),
        out_shape=jax.ShapeDtypeStruct((m, n), BF16),
        compiler_params=_params(("parallel", "arbitrary")),
        name="ffn_gate_up",
    )(a, wg, wu)


def _mm_resid_kernel(widths, *refs):
    a_refs = refs[:len(widths)]
    w_ref, x_ref, o_ref = refs[len(widths):]
    acc = x_ref[...]
    k0 = 0
    for a_ref, kw in zip(a_refs, widths):
        acc = acc + jnp.dot(a_ref[...], w_ref[k0:k0 + kw, :].astype(BF16), preferred_element_type=F32)
        k0 += kw
    o_ref[...] = acc


def _matmul_resid(a_list, w, layer, x, tm, tn, name):
    m = x.shape[0]
    _, k, n = w.shape
    widths = tuple(a.shape[1] for a in a_list)
    return pl.pallas_call(
        functools.partial(_mm_resid_kernel, widths),
        grid=(m // tm, n // tn),
        in_specs=[pl.BlockSpec((tm, kw), lambda i, j: (i, 0)) for kw in widths]
        + [_layer_cols(layer, k, tn, lambda j: j), pl.BlockSpec((tm, tn), lambda i, j: (i, j))],
        out_specs=pl.BlockSpec((tm, tn), lambda i, j: (i, j)),
        out_shape=jax.ShapeDtypeStruct((m, n), F32),
        compiler_params=_params(("parallel", "arbitrary")),
        name=name,
    )(*a_list, w, x)


def _rglru_kernel(nb, L, _y_all_ref, rx_ref, rg_ref, hist0_ref, h0_ref, cw_ref, cb_ref, wa_ref, ba_ref, wx_ref,
                  bx_ref, lam_ref, y_ref, h_ref, hist_ref, a_s, u_s):
    @pl.when(pl.program_id(1) == 0)
    def _():
        h_ref[...] = h0_ref[...]
        hist_ref[...] = hist0_ref[...]

    x = rx_ref[...]
    xc = _conv4(x, hist_ref[...], cw_ref[...], nb, L) + cb_ref[...]
    hist_ref[...] = _next_hist(x, nb, L)

    xcb = xc.astype(BF16)
    ga = jnp.concatenate([jnp.dot(xcb[:, n * R_BDIM:(n + 1) * R_BDIM], wa_ref[n], preferred_element_type=F32)
                          for n in range(R_BLOCKS)], axis=1)
    gx = jnp.concatenate([jnp.dot(xcb[:, n * R_BDIM:(n + 1) * R_BDIM], wx_ref[n], preferred_element_type=F32)
                          for n in range(R_BLOCKS)], axis=1)
    r = jax.nn.sigmoid(ga + ba_ref[...])
    i = jax.nn.sigmoid(gx + bx_ref[...])
    log_a = -R_C * r * _softplus(-lam_ref[...])
    a_s[...] = jnp.exp(log_a)
    th = jnp.tanh(log_a)
    u_s[...] = jnp.sqrt(-2.0 * th / (1.0 - th)) * (i * xc)

    if nb == 1:
        def step(t, h):
            h = a_s[pl.ds(t, 1), :] * h + u_s[pl.ds(t, 1), :]
            u_s[pl.ds(t, 1), :] = h
            return h

        h_last = lax.fori_loop(0, L, step, h_ref[0:1, :], unroll=SUBLANES)
        h_ref[...] = jnp.broadcast_to(h_last, (SUBLANES, R_WIDTH))
        hs = u_s[...]
    else:
        a, u = a_s[...], u_s[...]
        tpos = lax.broadcasted_iota(jnp.int32, (nb * L, 1), 0) % L
        step = 1
        while step < L:
            valid = tpos >= step
            u = jnp.where(valid, a * pltpu.roll(u, step, 0) + u, u)
            a = jnp.where(valid, a * pltpu.roll(a, step, 0), a)
            step *= 2
        if L == SUBLANES:
            h0 = h_ref[...]
        else:
            h0 = jnp.concatenate([jnp.broadcast_to(h_ref[s * SUBLANES:s * SUBLANES + 1, :], (L, R_WIDTH))
                                  for s in range(nb)], axis=0)
        hs = a * h0 + u
        if L == SUBLANES:
            h_ref[...] = pltpu.roll(hs, nb * L - (L - 1), 0)
        else:
            h_ref[...] = jnp.concatenate([jnp.broadcast_to(hs[(s + 1) * L - 1:(s + 1) * L, :], (SUBLANES, R_WIDTH))
                                          for s in range(nb)], axis=0)
    y_ref[...] = (hs * jax.nn.gelu(rg_ref[...])).astype(y_ref.dtype)


def _rglru_group(pm, y_all, row0, nseq, nb, L, nch, hist0, h0, prm):
    R = nb * L
    base = row0 // R
    nsb = nseq // nb
    rb = lambda sb, c: base + sb * nch + c
    full = lambda shape: pl.BlockSpec(shape, lambda sb, c: (0,) * len(shape))
    st = pl.BlockSpec((nb * SUBLANES, R_WIDTH), lambda sb, c: (sb, 0))
    return pl.pallas_call(
        functools.partial(_rglru_kernel, nb, L),
        grid=(nsb, nch),
        in_specs=[pl.BlockSpec(memory_space=pl.ANY),
                  pl.BlockSpec((R, R_WIDTH), lambda sb, c: (rb(sb, c), COL_RX // R_WIDTH)),
                  pl.BlockSpec((R, R_WIDTH), lambda sb, c: (rb(sb, c), COL_RG // R_WIDTH)),
                  st, st,
                  full((CONV_W, R_WIDTH)), full((1, R_WIDTH)),
                  full((R_BLOCKS, R_BDIM, R_BDIM)), full((1, R_WIDTH)),
                  full((R_BLOCKS, R_BDIM, R_BDIM)), full((1, R_WIDTH)), full((1, R_WIDTH))],
        out_specs=[pl.BlockSpec((R, R_WIDTH), lambda sb, c: (rb(sb, c), 0)), st, st],
        out_shape=[jax.ShapeDtypeStruct(y_all.shape, y_all.dtype),
                   jax.ShapeDtypeStruct((nseq * SUBLANES, R_WIDTH), F32),
                   jax.ShapeDtypeStruct((nseq * SUBLANES, R_WIDTH), F32)],
        scratch_shapes=[pltpu.VMEM((R, R_WIDTH), F32), pltpu.VMEM((R, R_WIDTH), F32)],
        compiler_params=_params(("parallel", "arbitrary")),
        input_output_aliases={0: 0},
        name="rglru",
    )(y_all, pm, pm, hist0, h0, *prm)


def _mlstm_kernel(nb, L, n_passthrough, *refs):
    (q_ref, k_ref, v_ref, o_ref, g_ref, bias_ref, nw_ref, c0_ref, n0_ref, m0_ref,
     y_ref, c_ref, n_ref, m_ref, num_s, col_s) = refs[n_passthrough:]
    R = nb * L

    @pl.when(pl.program_id(1) == 0)
    def _():
        c_ref[...] = c0_ref[...]
        n_ref[...] = n0_ref[...]
        m_ref[...] = m0_ref[...]

    z = g_ref[...] + bias_ref[...]
    lf = pltpu.roll(-_softplus(-z), LANES - GATE_MF, 1)
    causal, _ = _seq_masks(nb, L)
    fcum = _cumsum_rows(lf, nb, L)
    bcol = z - fcum
    brow = bcol.T
    if L == SUBLANES:
        m_rows = m_ref[...]
    else:
        m_rows = jnp.concatenate(
            [jnp.broadcast_to(m_ref[s * SUBLANES:s * SUBLANES + 1, :], (L, LANES)) for s in range(nb)], axis=0)

    for h in range(M_HEADS):
        fc = fcum[:, h:h + 1]
        d = fc + brow[h:h + 1, :]
        rowmax = jnp.max(jnp.where(causal, d, -jnp.inf), axis=-1, keepdims=True)
        inter = m_rows[:, h:h + 1] + fc
        mt = jnp.maximum(rowmax, inter)
        p = jnp.where(causal, jnp.exp(jnp.where(causal, d - mt, 0.0)), 0.0)
        qh = q_ref[:, h * M_DK:(h + 1) * M_DK] * (M_DK ** -0.5)
        s = _dot_nt(qh, k_ref[:, h * M_DK:(h + 1) * M_DK]) * p
        num_s[:, h * M_DV:(h + 1) * M_DV] = _dot(s, v_ref[:, h * M_DV:(h + 1) * M_DV])
        for idx, val in enumerate((mt, jnp.exp(inter - mt), jnp.sum(s, axis=-1, keepdims=True),
                                   bcol[:, h:h + 1], fc)):
            col_s[idx, h] = jnp.broadcast_to(val, (R, LANES))

    def per_seq(s):
        rows = _rows(s, L)
        mrow = pl.ds(s * SUBLANES, 1)
        for h in range(M_HEADS):
            dk = slice(h * M_DK, (h + 1) * M_DK)
            dv = slice(h * M_DV, (h + 1) * M_DV)
            mt = col_s[0, h, rows, 0:1]
            w_inter = col_s[1, h, rows, 0:1]
            den_in = col_s[2, h, rows, 0:1]
            bc = col_s[3, h, rows, 0:1]
            f_last = col_s[4, h, rows, 0:1][L - 1:L]
            qh = q_ref[rows, dk] * (M_DK ** -0.5)
            kh = k_ref[rows, dk]
            vh = v_ref[rows, dv]
            c_old = c_ref[s, h]
            n_old = n_ref[s, h:h + 1, :]
            m_old = m_ref[mrow, h:h + 1]
            num = w_inter * _dot(qh, c_old) + num_s[rows, dv]
            den = w_inter * jnp.sum(qh * n_old, axis=-1, keepdims=True) + den_in
            hh = num / jnp.maximum(jnp.abs(den), jnp.exp(-mt))
            hn = hh * lax.rsqrt(jnp.mean(hh * hh, axis=-1, keepdims=True) + NORM_EPS) * nw_ref[:, dv]
            y_ref[rows, dv] = (hn * jax.nn.sigmoid(o_ref[rows, dv])).astype(y_ref.dtype)
            m_new = mt[L - 1:L]
            kw = kh * jnp.exp(bc + (f_last - m_new))
            scale = jnp.exp(m_old + f_last - m_new)
            c_ref[s, h] = scale * c_old + _dot_tn(kw, vh)
            n_ref[s, h:h + 1, :] = scale * n_old + jnp.sum(kw, axis=0, keepdims=True)
            m_ref[pl.ds(s * SUBLANES, SUBLANES), h:h + 1] = jnp.broadcast_to(m_new, (SUBLANES, 1))

    _for_each_seq(nb, per_seq, MLSTM_SEQ_UNROLL)


def _stacked_state(layer, block):
    zeros = (0,) * (len(block) - 1)
    return pl.BlockSpec((None,) + block, lambda sb, c: (layer, sb) + zeros)


def _alias_inputs(y_all, state_out):
    prev, _, _ = state_out
    operands = [y_all] + ([prev] if prev is not None else [])
    return operands, {i: i for i in range(len(operands))}


def _mlstm_group(pm, pg, y_all, row0, nseq, nb, L, nch, c_in, n0, m0, bias_row, norm_row, c_out):
    R = nb * L
    base = row0 // R
    nsb = nseq // nb
    rb = lambda sb, c: base + sb * nch + c
    full = lambda shape: pl.BlockSpec(shape, lambda sb, c: (0,) * len(shape))
    cblock = (nb, M_HEADS, M_DK, M_DV)
    nst = pl.BlockSpec((nb, M_HEADS, M_DK), lambda sb, c: (sb, 0, 0))
    mst = pl.BlockSpec((nb * SUBLANES, LANES), lambda sb, c: (sb, 0))
    passthrough, aliases = _alias_inputs(y_all, c_out)
    return pl.pallas_call(
        functools.partial(_mlstm_kernel, nb, L, len(passthrough)),
        grid=(nsb, nch),
        in_specs=[pl.BlockSpec(memory_space=pl.ANY)] * len(passthrough) + [
                  pl.BlockSpec((R, M_HEADS * M_DK), lambda sb, c: (rb(sb, c), COL_MQ // (M_HEADS * M_DK))),
                  pl.BlockSpec((R, M_HEADS * M_DK), lambda sb, c: (rb(sb, c), COL_MK // (M_HEADS * M_DK))),
                  pl.BlockSpec((R, M_WIDTH), lambda sb, c: (rb(sb, c), COL_MV // M_WIDTH)),
                  pl.BlockSpec((R, M_WIDTH), lambda sb, c: (rb(sb, c), COL_MO // M_WIDTH)),
                  pl.BlockSpec((R, LANES), lambda sb, c: (rb(sb, c), 0)),
                  full((1, LANES)), full((1, M_WIDTH)), _stacked_state(c_in[1], cblock), nst, mst],
        out_specs=[pl.BlockSpec((R, M_WIDTH), lambda sb, c: (rb(sb, c), 0)),
                   _stacked_state(c_out[1], cblock), nst, mst],
        out_shape=[jax.ShapeDtypeStruct(y_all.shape, y_all.dtype),
                   jax.ShapeDtypeStruct((c_out[2], nseq, M_HEADS, M_DK, M_DV), F32),
                   jax.ShapeDtypeStruct((nseq, M_HEADS, M_DK), F32),
                   jax.ShapeDtypeStruct((nseq * SUBLANES, LANES), F32)],
        scratch_shapes=[pltpu.VMEM((R, M_WIDTH), F32), pltpu.VMEM((5, M_HEADS, R, LANES), F32)],
        compiler_params=_params(("parallel", "arbitrary")),
        input_output_aliases=aliases,
        name="mlstm",
    )(*passthrough, pm, pm, pm, pm, pg, bias_row, norm_row, c_in[0], n0, m0)


GDN_HEAD_GROUP = 16
GDN_SOLVE_ROWS = 64


def _gdn_kernel(nb, L, n_passthrough, *refs):
    (q_ref, k_ref, v_ref, z_ref, g_ref, cw_ref, hist0_ref, alog_ref, dtb_ref, nw_ref, s0_ref,
     y_ref, s_ref, hist_ref, k_s, kq_s, vb_s, t_s, at_s, ws_s, vn_s, g_s) = refs[n_passthrough:]
    R = nb * L

    @pl.when(pl.program_id(1) == 0)
    def _():
        s_ref[...] = s0_ref[...]
        hist_ref[...] = hist0_ref[...]

    zt = g_ref[...]
    beta = pltpu.roll(jax.nn.sigmoid(zt), LANES - GATE_GB, 1)
    gdec = pltpu.roll(-jnp.exp(alog_ref[...]) * _softplus(zt + dtb_ref[...]), LANES - GATE_GA, 1)
    causal, strict = _seq_masks(nb, L)
    gcum = _cumsum_rows(gdec, nb, L)
    g_s[...] = gcum
    grow = gcum.T

    eye = (lax.broadcasted_iota(jnp.int32, (R, R), 0) == lax.broadcasted_iota(jnp.int32, (R, R), 1)).astype(F32)
    block = min(L, GDN_SOLVE_ROWS)
    two_blocks = L > block
    assert L in (block, 2 * block)
    levels = int(math.log2(block)) - 1
    if two_blocks:
        diag_blocks = (lax.broadcasted_iota(jnp.int32, (R, R), 0) // block
                       == lax.broadcasted_iota(jnp.int32, (R, R), 1) // block)
    for h0 in range(0, G_HEADS, GDN_HEAD_GROUP):
        heads = range(h0, h0 + GDN_HEAD_GROUP)
        lhs, ks, decays = [], [], []
        for h in heads:
            sl = slice(h * G_DK, (h + 1) * G_DK)
            parts = []
            for idx, ref in enumerate((q_ref, k_ref, v_ref)):
                cs = slice(idx * G_WIDTH + h * G_DK, idx * G_WIDTH + (h + 1) * G_DK)
                parts.append(_silu(_conv4(ref[:, sl], hist_ref[:, cs], cw_ref[:, cs], nb, L)))
            qh, kh, vh = parts
            q = qh * lax.rsqrt(jnp.sum(qh * qh, axis=-1, keepdims=True) + NORM_EPS) * (G_DK ** -0.5)
            k = kh * lax.rsqrt(jnp.sum(kh * kh, axis=-1, keepdims=True) + NORM_EPS)
            bc = beta[:, h:h + 1]
            gc = gcum[:, h:h + 1]
            decays.append(jnp.where(causal, jnp.exp(jnp.where(causal, gc - grow[h:h + 1, :], 0.0)), 0.0))
            kb = k * bc
            eg = jnp.exp(gc)
            k_s[h] = k
            kq_s[h, 0] = kb * eg
            kq_s[h, 1] = q * eg
            vb_s[h] = vh * bc
            lhs.append(jnp.concatenate([kb, q], axis=0))
            ks.append(k)
        aas = [_dot_nt(l, k) for l, k in zip(lhs, ks)]
        avs = [jnp.where(strict, aa[:R] * d, 0.0) for aa, d in zip(aas, decays)]
        for h, aa, d in zip(heads, aas, decays):
            at_s[h] = aa[R:] * d
        if two_blocks:
            offs = [jnp.where(diag_blocks, 0.0, a) for a in avs]
            avs = [jnp.where(diag_blocks, a, 0.0) for a in avs]
        ps = [eye - a for a in avs]
        bs = [_dot(a, a) for a in avs]
        for lvl in range(levels):
            if lvl + 1 < levels:
                xs = [_dot(b, jnp.concatenate([b, p], axis=1)) for b, p in zip(bs, ps)]
                bs = [x[:, :R] for x in xs]
                ps = [p + x[:, R:] for p, x in zip(ps, xs)]
            else:
                ps = [p + _dot(b, p) for b, p in zip(bs, ps)]
        if two_blocks:
            ots = [_dot(o, p) for o, p in zip(offs, ps)]
            ps = [p - _dot(p, ot) for p, ot in zip(ps, ots)]
        for h, p in zip(heads, ps):
            t_s[h] = p

    def read_state(s):
        rows = _rows(s, L)
        for h in range(G_HEADS):
            x = _dot(jnp.concatenate([kq_s[h, 0, rows, :], kq_s[h, 1, rows, :]], axis=0), s_ref[s, h])
            ws_s[h, 0, rows, :] = x[:L]
            ws_s[h, 1, rows, :] = x[L:]

    _for_each_seq(nb, read_state, GDN_SEQ_UNROLL)

    for h in range(G_HEADS):
        sl = slice(h * G_DK, (h + 1) * G_DK)
        v_new = _dot(t_s[h], vb_s[h] - ws_s[h, 0])
        vn_s[h] = v_new
        o = ws_s[h, 1] + _dot(at_s[h], v_new)
        on = o * lax.rsqrt(jnp.mean(o * o, axis=-1, keepdims=True) + NORM_EPS) * nw_ref[...]
        y_ref[:, sl] = (on * _silu(z_ref[:, sl])).astype(y_ref.dtype)

    def write_state(s):
        rows = _rows(s, L)
        for h in range(G_HEADS):
            gseq = g_s[rows, h:h + 1]
            g_end = gseq[L - 1:L]
            kdec = k_s[h, rows, :] * jnp.exp(g_end - gseq)
            s_ref[s, h] = s_ref[s, h] * jnp.exp(g_end) + _dot_tn(kdec, vn_s[h, rows, :])

    _for_each_seq(nb, write_state, GDN_SEQ_UNROLL)

    for idx, ref in enumerate((q_ref, k_ref, v_ref)):
        hist_ref[:, idx * G_WIDTH:(idx + 1) * G_WIDTH] = _next_hist(ref[...], nb, L)


def _gdn_group(pm, pg, y_all, row0, nseq, nb, L, nch, s_in, hist0, prm, s_out):
    R = nb * L
    base = row0 // R
    nsb = nseq // nb
    rb = lambda sb, c: base + sb * nch + c
    full = lambda shape: pl.BlockSpec(shape, lambda sb, c: (0,) * len(shape))
    sblock = (nb, G_HEADS, G_DK, G_DV)
    hst = pl.BlockSpec((nb * SUBLANES, G_QKV), lambda sb, c: (sb, 0))
    col = lambda off: pl.BlockSpec((R, G_WIDTH), lambda sb, c: (rb(sb, c), off // G_WIDTH))
    passthrough, aliases = _alias_inputs(y_all, s_out)
    return pl.pallas_call(
        functools.partial(_gdn_kernel, nb, L, len(passthrough)),
        grid=(nsb, nch),
        in_specs=[pl.BlockSpec(memory_space=pl.ANY)] * len(passthrough) + [
                  col(COL_GQ), col(COL_GK), col(COL_GV), col(COL_GZ),
                  pl.BlockSpec((R, LANES), lambda sb, c: (rb(sb, c), 0)),
                  full((CONV_W, G_QKV)), hst, full((1, LANES)), full((1, LANES)), full((1, G_DV)),
                  _stacked_state(s_in[1], sblock)],
        out_specs=[pl.BlockSpec((R, G_WIDTH), lambda sb, c: (rb(sb, c), 0)),
                   _stacked_state(s_out[1], sblock), hst],
        out_shape=[jax.ShapeDtypeStruct(y_all.shape, y_all.dtype),
                   jax.ShapeDtypeStruct((s_out[2], nseq, G_HEADS, G_DK, G_DV), F32),
                   jax.ShapeDtypeStruct((nseq * SUBLANES, G_QKV), F32)],
        scratch_shapes=[pltpu.VMEM((G_HEADS, R, G_DK), F32), pltpu.VMEM((G_HEADS, 2, R, G_DK), F32),
                        pltpu.VMEM((G_HEADS, R, G_DV), F32), pltpu.VMEM((G_HEADS, R, R), F32),
                        pltpu.VMEM((G_HEADS, R, R), F32), pltpu.VMEM((G_HEADS, 2, R, G_DV), F32),
                        pltpu.VMEM((G_HEADS, R, G_DV), F32), pltpu.VMEM((R, LANES), F32)],
        compiler_params=_params(("parallel", "arbitrary")),
        input_output_aliases=aliases,
        name="gdn",
    )(*passthrough, pm, pm, pm, pm, pg, prm[0], hist0, prm[1], prm[2], prm[3], s_in[0])


def _lane_row(vec, offset):
    return jnp.zeros((1, LANES), F32).at[0, offset:offset + vec.shape[0]].set(vec.astype(F32))


def _pad_hist(buf):
    n, w, c = buf.shape
    return jnp.pad(buf.astype(F32), ((0, 0), (0, SUBLANES - w), (0, 0))).reshape(n * SUBLANES, c)


def _unpad_hist(hist, n):
    return hist.reshape(n, SUBLANES, hist.shape[-1])[:, :CONV_W - 1]


def _mixers(proj, mix, group, states, big_out, lw):
    pa, ps, pg = proj
    row0, nseq, nb_m, nb_r, nb_g, L_m, L_r, L_g, T = group
    c_in, n0, m0, h0, rhist, s_in, ghist = states
    ym, c, n, m = _mlstm_group(pa, pg, mix[0], row0, nseq, nb_m, L_m, T // L_m, c_in, n0, m0,
                               lw["m_bias"], lw["m_norm"], big_out[0])
    yr, h, rh = _rglru_group(ps, mix[1], row0, nseq, nb_r, L_r, T // L_r, rhist, h0, lw["r_prm"])
    yg, s, gh = _gdn_group(ps, pg, mix[2], row0, nseq, nb_g, L_g, T // L_g, s_in, ghist, lw["g_prm"], big_out[1])
    return (ym, yr, yg), (c, n, m, h, rh, s, gh)


def kernel(x_prompt, x_sample, state_mlstm_C, state_mlstm_n, state_mlstm_m, state_rglru_h, state_rglru_conv, state_gdn_S, state_gdn_conv, meta_tokens, norm_mix, w_in, m_bias_i, m_bias_f, m_norm, r_conv_w, r_conv_b, r_gate_a_w, r_gate_a_b, r_gate_x_w, r_gate_x_b, r_lambda, g_conv_w, g_A_log, g_dt_bias, g_norm, w_out, norm_ffn, w_gate, w_up, w_down, norm_final):
    batch, seq, d = x_prompt.shape
    dec_batch, dec_seq, _ = x_sample.shape
    depth = w_in.shape[0]
    n_prompt = batch * seq
    n_sample = dec_batch * dec_seq
    n_rows = n_prompt + n_sample + batch * N_META

    meta = jnp.broadcast_to(meta_tokens.astype(F32)[None], (batch, N_META, d))
    x = jnp.concatenate([x_prompt.reshape(n_prompt, d), x_sample.reshape(n_sample, d),
                         meta.reshape(batch * N_META, d)], axis=0)

    g_meta = (n_prompt + n_sample, batch, batch, batch, batch, N_META, N_META, N_META, N_META)
    g_prompt = (0, batch, 1, 1, 1, 512, 256, 128, seq)
    g_sample = (n_prompt, dec_batch, 16, 32, 8, dec_seq, dec_seq, dec_seq, dec_seq)

    zeros = lambda *shape: jnp.zeros(shape, F32)
    meta_states = ((zeros(1, batch, M_HEADS, M_DK, M_DV), 0), zeros(batch, M_HEADS, M_DK),
                   zeros(batch * SUBLANES, LANES), zeros(batch * SUBLANES, R_WIDTH), zeros(batch * SUBLANES, R_WIDTH),
                   (zeros(1, batch, G_HEADS, G_DK, G_DV), 0), zeros(batch * SUBLANES, G_QKV))
    one_layer = ((None, 0, 1), (None, 0, 1))
    w_down_bf16 = w_down.astype(BF16)
    w_in_t = jnp.swapaxes(w_in, 1, 2)

    prompt_new, sample_new = [], []
    sample_c = sample_s = None
    for l in range(depth):
        lw = dict(
            m_bias=_lane_row(m_bias_i[l], GATE_MI) + _lane_row(m_bias_f[l], GATE_MF),
            m_norm=m_norm[l].reshape(1, M_WIDTH),
            r_prm=(r_conv_w[l], r_conv_b[l].reshape(1, R_WIDTH), r_gate_a_w[l].astype(BF16),
                   r_gate_a_b[l].reshape(1, R_WIDTH), r_gate_x_w[l].astype(BF16),
                   r_gate_x_b[l].reshape(1, R_WIDTH), r_lambda[l].reshape(1, R_WIDTH)),
            g_prm=(g_conv_w[l], _lane_row(g_A_log[l], GATE_GA), _lane_row(g_dt_bias[l], GATE_GA),
                   g_norm[l].reshape(1, G_DV)),
        )
        sample_states = ((state_mlstm_C, l), state_mlstm_n[l],
                         jnp.pad(jnp.repeat(state_mlstm_m[l], SUBLANES, axis=0), ((0, 0), (0, LANES - M_HEADS))),
                         jnp.repeat(state_rglru_h[l], SUBLANES, axis=0), _pad_hist(state_rglru_conv[l]),
                         (state_gdn_S, l), _pad_hist(state_gdn_conv[l]))

        hn = _rmsnorm(x, norm_mix[l], BF16)
        proj = _in_proj(hn, w_in_t, l)

        def as_input(st):
            return ((st[0], 0),) + st[1:5] + ((st[5], 0), st[6])

        mix = tuple(jnp.zeros((n_rows, width), BF16) for width in (M_WIDTH, R_WIDTH, G_WIDTH))
        mix, st_meta = _mixers(proj, mix, g_meta, meta_states, one_layer, lw)
        mix, st_prompt = _mixers(proj, mix, g_prompt, as_input(st_meta), one_layer, lw)
        mix, st_sample = _mixers(proj, mix, g_sample, sample_states,
                                 ((sample_c, l, depth), (sample_s, l, depth)), lw)
        sample_c, sample_s = st_sample[0], st_sample[5]
        prompt_new.append(st_prompt)
        sample_new.append(st_sample)

        x = _matmul_resid(mix, w_out, l, x, ROWS_DENSE, COLS_DENSE, "out_proj")
        hf = _rmsnorm(x, norm_ffn[l], BF16)
        hmid = _matmul_swiglu(hf, w_gate, w_up, l)
        x = _matmul_resid([hmid], w_down_bf16, l, x, ROWS_DOWN, COLS_DOWN, "ffn_down")

    y_prompt_out, y_sample_out = _final_norm(x, norm_final, n_prompt, n_sample)
    y_prompt_out = y_prompt_out.reshape(batch, seq, d)
    y_sample_out = y_sample_out.reshape(dec_batch, dec_seq, d)

    def small_states(per_layer, n):
        nn, m, h, rh, gh = (jnp.stack([st[i] for st in per_layer]) for i in (1, 2, 3, 4, 6))
        return (nn, m[:, ::SUBLANES, :M_HEADS], h[:, ::SUBLANES],
                jax.vmap(lambda a: _unpad_hist(a, n))(rh), jax.vmap(lambda a: _unpad_hist(a, n))(gh))

    pn, pm_, ph, prh, pgh = small_states(prompt_new, batch)
    sn, sm, sh, srh, sgh = small_states(sample_new, dec_batch)
    prompt_c = jnp.concatenate([st[0] for st in prompt_new], axis=0)
    prompt_s = jnp.concatenate([st[5] for st in prompt_new], axis=0)
    return (y_prompt_out, y_sample_out, prompt_c, pn, pm_, ph, prh, prompt_s, pgh,
            sample_c, sn, sm, sh, srh, sample_s, sgh)
```
